```python
import jax
import jax.numpy as jnp
from jax import lax
import numpy as np

D_MODEL = 4096
BATCH = 1
SEQ = 8192
DEPTH = 4

GRID_W = 64
CTX_LEN = 256
N_MIXERS = 4
N_MOD = 6
ROPE_THETA = 10000.0
EPS = 1e-6

POOL_WINDOWS = (2, 4, 8, 16)
POOL_GROUPS = len(POOL_WINDOWS)
POOL_CG = D_MODEL // POOL_GROUPS

ATTN_HEAD_DIM = 128
ATTN_Q_HEADS = D_MODEL // ATTN_HEAD_DIM
ATTN_KV_HEADS = ATTN_Q_HEADS // 4
ATTN_GROUP = ATTN_Q_HEADS // ATTN_KV_HEADS
Q_BLOCK = 128

MLSTM_HEADS = 8
MLSTM_QK_DIM = D_MODEL // 2 // MLSTM_HEADS
MLSTM_V_DIM = D_MODEL // MLSTM_HEADS
MLSTM_CHUNK = 64

RET_HEADS = 16
RET_HEAD_DIM = D_MODEL // RET_HEADS
RET_CHUNK = 128

D_FF = 3 * D_MODEL // 2
CONV_WIDTH = 3

kernel_name = 'hybrid_pool_gqa_mlstm_retention_prefix_dit'


def layers_of_type(kind):
    return len(range(kind, DEPTH, N_MIXERS))


def rms_norm(x, g):
    x32 = x.astype(jnp.float32)
    y = x32 * lax.rsqrt(jnp.mean(x32 * x32, axis=-1, keepdims=True) + EPS)
    return (y * g.astype(jnp.float32)).astype(x.dtype)


def modulate(z, g, shift, scale):
    return rms_norm(z, g) * (1 + scale) + shift


def flip_seq(a):
    return jnp.flip(a, axis=1)


def axial_rope_tables(rows, head_dim):
    row = jnp.repeat(jnp.arange(rows), GRID_W).astype(jnp.float32)
    col = jnp.tile(jnp.arange(GRID_W), rows).astype(jnp.float32)
    axis_dim = head_dim // 2
    inv_freq = 1.0 / (ROPE_THETA ** (jnp.arange(0, axis_dim, 2, dtype=jnp.float32) / axis_dim))
    ang = jnp.concatenate([row[:, None] * inv_freq, col[:, None] * inv_freq], axis=-1)
    return jnp.cos(ang), jnp.sin(ang)


def apply_rope(x, cos, sin):
    xf = x.astype(jnp.float32).reshape(*x.shape[:-1], -1, 2)
    x0, x1 = xf[..., 0], xf[..., 1]
    c = cos[None, :, None, :]
    s = sin[None, :, None, :]
    out = jnp.stack([x0 * c - x1 * s, x0 * s + x1 * c], axis=-1)
    return out.reshape(x.shape).astype(x.dtype)


def to_chunks(a, size):
    b, t = a.shape[:2]
    a = a.reshape(b, t // size, size, *a.shape[2:])
    return jnp.moveaxis(jnp.moveaxis(a, 1, 0), 3, 2)


def from_chunks(a):
    a = jnp.moveaxis(jnp.moveaxis(a, 2, 3), 0, 1)
    return a.reshape(a.shape[0], a.shape[1] * a.shape[2], *a.shape[3:])


def centred_mean_minus_self(h, window):
    t_len = h.shape[1]
    h32 = h.astype(jnp.float32)
    csum = jnp.pad(jnp.cumsum(h32, axis=1), ((0, 0), (1, 0), (0, 0)))
    t = jnp.arange(t_len)
    lo = jnp.clip(t - window // 2, 0, t_len)
    hi = jnp.clip(t + window - window // 2, 0, t_len)
    cnt = (hi - lo).astype(jnp.float32)
    mean = (jnp.take(csum, hi, axis=1) - jnp.take(csum, lo, axis=1)) / cnt[None, :, None]
    return (mean - h32).astype(h.dtype)


def pool_mixer(h, w, scale):
    b, t, _ = h.shape
    hg = h.reshape(b, t, POOL_GROUPS, POOL_CG)
    pooled = jnp.stack([centred_mean_minus_self(hg[:, :, g], POOL_WINDOWS[g]) for g in range(POOL_GROUPS)], axis=2)
    y = jnp.einsum('btgc,gcd->btgd', pooled, w).reshape(b, t, D_MODEL)
    return y * scale


def attn_mixer(h, hc, wq, wk, wv, wo, q_norm, k_norm, rows, need_ctx):
    b, t, _ = h.shape
    cos, sin = axial_rope_tables(rows, ATTN_HEAD_DIM)
    scale = ATTN_HEAD_DIM ** -0.5

    def qkv(z):
        n = z.shape[1]
        q = rms_norm((z @ wq).reshape(b, n, ATTN_Q_HEADS, ATTN_HEAD_DIM), q_norm)
        k = rms_norm((z @ wk).reshape(b, n, ATTN_KV_HEADS, ATTN_HEAD_DIM), k_norm)
        v = (z @ wv).reshape(b, n, ATTN_KV_HEADS, ATTN_HEAD_DIM)
        return q, k, v

    q, k, v = qkv(h)
    q, k = apply_rope(q, cos, sin), apply_rope(k, cos, sin)
    qc, kc, vc = qkv(hc)

    def attend(qb, keys, vals):
        nq = qb.shape[1]
        qg = qb.astype(jnp.float32).reshape(b, nq, ATTN_KV_HEADS, ATTN_GROUP, ATTN_HEAD_DIM)
        p = jax.nn.softmax(jnp.einsum('bqhgd,bkhd->bhgqk', qg, keys) * scale, axis=-1)
        o = jnp.einsum('bhgqk,bkhd->bqhgd', p, vals)
        return o.reshape(b, nq, ATTN_Q_HEADS * ATTN_HEAD_DIM).astype(qb.dtype)

    keys = jnp.concatenate([k, kc], axis=1).astype(jnp.float32)
    vals = jnp.concatenate([v, vc], axis=1).astype(jnp.float32)
    n_blk = t // Q_BLOCK
    q_blocks = jnp.moveaxis(q.reshape(b, n_blk, Q_BLOCK, ATTN_Q_HEADS, ATTN_HEAD_DIM), 1, 0)
    o = lax.map(lambda qb: attend(qb, keys, vals), q_blocks)
    y = jnp.moveaxis(o, 0, 1).reshape(b, t, ATTN_Q_HEADS * ATTN_HEAD_DIM) @ wo
    yc = None
    if need_ctx:
        yc = attend(qc, kc.astype(jnp.float32), vc.astype(jnp.float32)) @ wo
    return y, yc


def mlstm_scan(q, k, v, i_pre, log_f, state):
    tri = jnp.tril(jnp.ones((MLSTM_CHUNK, MLSTM_CHUNK), dtype=bool))

    def step(carry, xs):
        c_mat, n_vec, m = carry
        qc, kc, vc, ic, fc = xs
        bsum = jnp.cumsum(fc, axis=-1)
        log_d = jnp.where(tri, bsum[..., :, None] - bsum[..., None, :] + ic[..., None, :], -jnp.inf)
        log_prev = bsum + m[..., None]
        m_t = jnp.maximum(log_prev, jnp.max(log_d, axis=-1))
        s = jnp.einsum('bhtd,bhsd->bhts', qc, kc) * jnp.exp(log_d - m_t[..., None])
        w_prev = jnp.exp(log_prev - m_t)
        num = jnp.einsum('bhts,bhsv->bhtv', s, vc) + w_prev[..., None] * jnp.einsum('bhtd,bhvd->bhtv', qc, c_mat)
        den = jnp.sum(s, axis=-1) + w_prev * jnp.einsum('bhtd,bhd->bht', qc, n_vec)
        h_out = num / jnp.maximum(jnp.abs(den), jnp.exp(-m_t))[..., None]
        b_last = bsum[..., -1]
        log_w = b_last[..., None] - bsum + ic
        m_new = jnp.maximum(b_last + m, jnp.max(log_w, axis=-1))
        w = jnp.exp(log_w - m_new[..., None])
        decay = jnp.exp(b_last + m - m_new)
        c_new = decay[..., None, None] * c_mat + jnp.einsum('bhs,bhsv,bhsd->bhvd', w, vc, kc)
        n_new = decay[..., None] * n_vec + jnp.einsum('bhs,bhsd->bhd', w, kc)
        return (c_new, n_new, m_new), h_out

    xs = (to_chunks(q, MLSTM_CHUNK), to_chunks(k, MLSTM_CHUNK), to_chunks(v, MLSTM_CHUNK),
          to_chunks(i_pre, MLSTM_CHUNK), to_chunks(log_f, MLSTM_CHUNK))
    state, hs = lax.scan(step, state, xs)
    return state, from_chunks(hs)


def mlstm_mixer(h, hc, wq, wk, wv, w_gates, b_gates, w_ogate, out_norm, wo, need_ctx):
    b = h.shape[0]
    nh, dk, dv = MLSTM_HEADS, MLSTM_QK_DIM, MLSTM_V_DIM

    def feats(z):
        n = z.shape[1]
        q = (z @ wq).reshape(b, n, nh, dk).astype(jnp.float32)
        k = (z @ wk).reshape(b, n, nh, dk).astype(jnp.float32) * dk ** -0.5
        v = (z @ wv).reshape(b, n, nh, dv).astype(jnp.float32)
        g = (z @ w_gates + b_gates).astype(jnp.float32).reshape(b, n, 2, 2, nh)
        return q, k, v, g[:, :, :, 0], jax.nn.log_sigmoid(g[:, :, :, 1])

    q, k, v, ig, lf = feats(h)
    qc, kc, vc, igc, lfc = feats(hc)
    zero = (jnp.zeros((b, nh, dv, dk), jnp.float32), jnp.zeros((b, nh, dk), jnp.float32),
            jnp.zeros((b, nh), jnp.float32))
    st_f, hc_f = mlstm_scan(qc, kc, vc, igc[:, :, 0], lfc[:, :, 0], zero)
    st_b, hc_b = mlstm_scan(flip_seq(qc), flip_seq(kc), flip_seq(vc), flip_seq(igc[:, :, 1]), flip_seq(lfc[:, :, 1]), zero)
    _, h_f = mlstm_scan(q, k, v, ig[:, :, 0], lf[:, :, 0], st_f)
    _, h_b = mlstm_scan(flip_seq(q), flip_seq(k), flip_seq(v), flip_seq(ig[:, :, 1]), flip_seq(lf[:, :, 1]), st_b)

    def out(z, hh):
        hh = rms_norm(hh, out_norm.reshape(nh, dv)).astype(z.dtype).reshape(b, z.shape[1], D_MODEL)
        return (jax.nn.sigmoid(z @ w_ogate) * hh) @ wo

    y = out(h, h_f + flip_seq(h_b))
    yc = out(hc, hc_f + flip_seq(hc_b)) if need_ctx else None
    return y, yc


def retention_scan(q, k, v, log_gamma, state):
    pos = jnp.arange(RET_CHUNK, dtype=jnp.float32)
    rel = pos[:, None] - pos[None, :]
    causal_in_chunk = rel >= 0
    decay_in = jnp.where(causal_in_chunk, jnp.exp(log_gamma[:, None, None] * jnp.where(causal_in_chunk, rel, 0.0)), 0.0)
    xi = jnp.exp(log_gamma[:, None] * (pos + 1.0))
    zeta = jnp.exp(log_gamma[:, None] * (RET_CHUNK - 1.0 - pos))
    gamma_chunk = jnp.exp(log_gamma * RET_CHUNK)

    def step(s_state, xs):
        qc, kc, vc = xs
        sc = jnp.einsum('bhtd,bhsd->bhts', qc, kc) * decay_in
        o = jnp.einsum('bhts,bhsv->bhtv', sc, vc) + jnp.einsum('bhtd,bhdv->bhtv', qc, s_state) * xi[..., None]
        s_new = gamma_chunk[:, None, None] * s_state + jnp.einsum('bhsd,bhsv->bhdv', kc * zeta[..., None], vc)
        return s_new, o

    xs = (to_chunks(q, RET_CHUNK), to_chunks(k, RET_CHUNK), to_chunks(v, RET_CHUNK))
    state, os_ = lax.scan(step, state, xs)
    return state, from_chunks(os_)


def retention_mixer(h, hc, wq, wk, wv, wg, decay_logit, out_norm, wo, rows, need_ctx):
    b = h.shape[0]
    nh, hd = RET_HEADS, RET_HEAD_DIM
    cos, sin = axial_rope_tables(rows, hd)

    def feats(z, rotate):
        n = z.shape[1]
        q = (z @ wq).reshape(b, n, nh, hd)
        k = (z @ wk).reshape(b, n, nh, hd)
        if rotate:
            q, k = apply_rope(q, cos, sin), apply_rope(k, cos, sin)
        v = (z @ wv).reshape(b, n, nh, hd)
        return q.astype(jnp.float32), k.astype(jnp.float32) * hd ** -0.5, v.astype(jnp.float32)

    log_gamma = jax.nn.log_sigmoid(decay_logit.astype(jnp.float32))
    q, k, v = feats(h, True)
    qc, kc, vc = feats(hc, False)
    zero = jnp.zeros((b, nh, hd, hd), jnp.float32)
    s_cf, oc_f = retention_scan(qc, kc, vc, log_gamma[0], zero)
    s_cb, oc_b = retention_scan(flip_seq(qc), flip_seq(kc), flip_seq(vc), log_gamma[1], zero)
    _, o_f = retention_scan(q, k, v, log_gamma[0], s_cf)
    _, o_b = retention_scan(flip_seq(q), flip_seq(k), flip_seq(v), log_gamma[1], s_cb)

    def out(z, o):
        o = rms_norm(o, out_norm.reshape(nh, hd)).astype(z.dtype).reshape(b, z.shape[1], D_MODEL)
        return (jax.nn.silu(z @ wg) * o) @ wo

    y = out(h, o_f + flip_seq(o_b))
    yc = out(hc, oc_f + flip_seq(oc_b)) if need_ctx else None
    return y, yc


def conv_ffn(h, w_up, conv_w, conv_b, w_down):
    t = h.shape[1]
    u = h @ w_up
    half = CONV_WIDTH // 2
    up = jnp.pad(u, ((0, 0), (half, half), (0, 0)))
    u = sum(up[:, j:j + t] * conv_w[j] for j in range(CONV_WIDTH)) + conv_b
    gate, val = jnp.split(u, 2, axis=-1)
    return (jax.nn.silu(gate) * val) @ w_down


def setup_inputs(seed: int = 0) -> dict:
    key = jax.random.key(seed)
    ks = iter(jax.random.split(key, 40))
    f32 = jnp.float32
    dm = D_MODEL

    def nrm(shape, scale):
        return jax.random.normal(next(ks), shape, f32) * scale

    n_a, n_b, n_c, n_d = (layers_of_type(kind) for kind in range(N_MIXERS))
    f_base = np.concatenate([np.zeros(MLSTM_HEADS), np.linspace(3.0, 6.0, MLSTM_HEADS),
                             np.zeros(MLSTM_HEADS), np.linspace(3.0, 6.0, MLSTM_HEADS)]).astype(np.float32)
    e = 5.0 + np.arange(RET_HEADS, dtype=np.float32)
    gamma0 = 1.0 - 2.0 ** (-e)
    decay_base = (np.log(gamma0) - np.log(1.0 - gamma0)).astype(np.float32)
    conv_base = jnp.asarray(np.array([0.25, 0.5, 0.25], dtype=np.float32))[None, :, None]
    return {
        'x': nrm((BATCH, SEQ, dm), 1.0),
        'c': nrm((BATCH, dm), 1.0),
        'ctx': nrm((BATCH, CTX_LEN, dm), 1.0),
        'c_ctx': nrm((dm,), 1.0),
        'ada_w': nrm((DEPTH, dm, N_MOD * dm), 0.5 * dm ** -0.5),
        'ada_b': nrm((DEPTH, N_MOD * dm), 0.02),
        'norm1_g': 1.0 + nrm((DEPTH, dm), 0.02),
        'norm2_g': 1.0 + nrm((DEPTH, dm), 0.02),
        'ffn_w_up': nrm((DEPTH, dm, 2 * D_FF), dm ** -0.5),
        'ffn_conv_w': conv_base + nrm((DEPTH, CONV_WIDTH, 2 * D_FF), 0.1),
        'ffn_conv_b': nrm((DEPTH, 2 * D_FF), 0.01),
        'ffn_w_down': nrm((DEPTH, D_FF, dm), D_FF ** -0.5),
        'pool_w': nrm((n_a, POOL_GROUPS, POOL_CG, POOL_CG), POOL_CG ** -0.5),
        'pool_scale': 1.0 + nrm((n_a, dm), 0.02),
        'attn_wq': nrm((n_b, dm, ATTN_Q_HEADS * ATTN_HEAD_DIM), dm ** -0.5),
        'attn_wk': nrm((n_b, dm, ATTN_KV_HEADS * ATTN_HEAD_DIM), dm ** -0.5),
        'attn_wv': nrm((n_b, dm, ATTN_KV_HEADS * ATTN_HEAD_DIM), dm ** -0.5),
        'attn_wo': nrm((n_b, ATTN_Q_HEADS * ATTN_HEAD_DIM, dm), (ATTN_Q_HEADS * ATTN_HEAD_DIM) ** -0.5),
        'attn_q_norm': 1.0 + nrm((n_b, ATTN_HEAD_DIM), 0.02),
        'attn_k_norm': 1.0 + nrm((n_b, ATTN_HEAD_DIM), 0.02),
        'mlstm_wq': nrm((n_c, dm, MLSTM_HEADS * MLSTM_QK_DIM), dm ** -0.5),
        'mlstm_wk': nrm((n_c, dm, MLSTM_HEADS * MLSTM_QK_DIM), dm ** -0.5),
        'mlstm_wv': nrm((n_c, dm, MLSTM_HEADS * MLSTM_V_DIM), dm ** -0.5),
        'mlstm_w_gates': nrm((n_c, dm, 4 * MLSTM_HEADS), dm ** -0.5),
        'mlstm_b_gates': jnp.asarray(f_base)[None, :] + nrm((n_c, 4 * MLSTM_HEADS), 0.1),
        'mlstm_w_ogate': nrm((n_c, dm, dm), dm ** -0.5),
        'mlstm_out_norm': 1.0 + nrm((n_c, dm), 0.02),
        'mlstm_wo': nrm((n_c, dm, dm), dm ** -0.5),
        'ret_wq': nrm((n_d, dm, dm), dm ** -0.5),
        'ret_wk': nrm((n_d, dm, dm), dm ** -0.5),
        'ret_wv': nrm((n_d, dm, dm), dm ** -0.5),
        'ret_wg': nrm((n_d, dm, dm), dm ** -0.5),
        'ret_decay_logit': jnp.asarray(decay_base)[None, None, :] + nrm((n_d, 2, RET_HEADS), 0.01),
        'ret_out_norm': 1.0 + nrm((n_d, dm), 0.02),
        'ret_wo': nrm((n_d, dm, dm), dm ** -0.5),
    }


def reference(x, c, ctx, c_ctx, ada_w, ada_b, norm1_g, norm2_g, ffn_w_up, ffn_conv_w, ffn_conv_b, ffn_w_down,
              pool_w, pool_scale, attn_wq, attn_wk, attn_wv, attn_wo, attn_q_norm, attn_k_norm,
              mlstm_wq, mlstm_wk, mlstm_wv, mlstm_w_gates, mlstm_b_gates, mlstm_w_ogate, mlstm_out_norm, mlstm_wo,
              ret_wq, ret_wk, ret_wv, ret_wg, ret_decay_logit, ret_out_norm, ret_wo):
    n_tokens = x.shape[1]
    rows = n_tokens // GRID_W
    xc = ctx
    for i in range(DEPTH):
        kind, j = i % N_MIXERS, i // N_MIXERS
        need_ctx = i < DEPTH - 1
        mod = (jax.nn.silu(c) @ ada_w[i] + ada_b[i])[:, None, :]
        mod_c = (jax.nn.silu(c_ctx) @ ada_w[i] + ada_b[i])[None, None, :]
        sh1, sc1, g1, sh2, sc2, g2 = jnp.split(mod, N_MOD, axis=-1)
        csh1, csc1, cg1, csh2, csc2, cg2 = jnp.split(mod_c, N_MOD, axis=-1)
        h = modulate(x, norm1_g[i], sh1, sc1)
        hc = modulate(xc, norm1_g[i], csh1, csc1)
        if kind == 0:
            y = pool_mixer(h, pool_w[j], pool_scale[j])
            yc = pool_mixer(hc, pool_w[j], pool_scale[j]) if need_ctx else None
        elif kind == 1:
            y, yc = attn_mixer(h, hc, attn_wq[j], attn_wk[j], attn_wv[j], attn_wo[j],
                               attn_q_norm[j], attn_k_norm[j], rows, need_ctx)
        elif kind == 2:
            y, yc = mlstm_mixer(h, hc, mlstm_wq[j], mlstm_wk[j], mlstm_wv[j], mlstm_w_gates[j], mlstm_b_gates[j],
                                mlstm_w_ogate[j], mlstm_out_norm[j], mlstm_wo[j], need_ctx)
        else:
            y, yc = retention_mixer(h, hc, ret_wq[j], ret_wk[j], ret_wv[j], ret_wg[j], ret_decay_logit[j],
                                    ret_out_norm[j], ret_wo[j], rows, need_ctx)
        x = x + g1 * y
        x = x + g2 * conv_ffn(modulate(x, norm2_g[i], sh2, sc2), ffn_w_up[i], ffn_conv_w[i], ffn_conv_b[i], ffn_w_down[i])
        if need_ctx:
            xc = xc + cg1 * yc
            xc = xc + cg2 * conv_ffn(modulate(xc, norm2_g[i], csh2, csc2), ffn_w_up[i], ffn_conv_w[i], ffn_conv_b[i], ffn_w_down[i])
    return x
```

```python
import functools

import jax
import jax.numpy as jnp
from jax import lax
from jax.experimental import pallas as pl
from jax.experimental.pallas import tpu as pltpu

F32 = jnp.float32
BF16 = jnp.bfloat16

EPS = 1e-6
GRID_W = 64
ROPE_THETA = 10000.0
N_MOD = 6
POOL_WINDOWS = (2, 4, 8, 16)
ATTN_HEAD_DIM = 128
ATTN_GROUP = 4
MLSTM_HEADS = 8
RET_HEADS = 16
CONV_WIDTH = 3

VMEM_LIMIT_BYTES = 56 * 1024 * 1024
BF16_SUBLANES = 16
LANES = 128
SCAN_CHUNK = 256
POOL_HALO = 8


def _pick(n, candidates):
    for c in candidates:
        if c <= n and n % c == 0:
            return c
    return n


def _params(sem):
    return pltpu.CompilerParams(dimension_semantics=sem, vmem_limit_bytes=VMEM_LIMIT_BYTES)


def _silu(x):
    return x * (1.0 / (1.0 + jnp.exp(-x)))


def _sigmoid(x):
    return 1.0 / (1.0 + jnp.exp(-x))


def _log_sigmoid(x):
    return jnp.minimum(x, 0.0) - jnp.log(1.0 + jnp.exp(-jnp.abs(x)))


def _row_ids(i, tm):
    return i * tm + lax.broadcasted_iota(jnp.int32, (tm, 1), 0)


def _sel_rows(is_ctx, ref2):
    return jnp.where(is_ctx, ref2[1:2, :], ref2[0:1, :])


def _dot(a, b):
    return jnp.dot(a, b, preferred_element_type=F32)


def _ada_kernel(a_ref, w_ref, b_ref, o_ref):
    a = _silu(a_ref[...]).astype(BF16)
    o_ref[...] = _dot(a, w_ref[...].astype(BF16)) + b_ref[...]


def _ada_mods(cc, ada_w, ada_b):
    depth, d, n = ada_w.shape
    tn = _pick(n, (512, 256, 128))
    rows = cc.shape[0]
    return pl.pallas_call(
        _ada_kernel,
        grid=(depth, n // tn),
        in_specs=[
            pl.BlockSpec((rows, d), lambda l, j: (0, 0)),
            pl.BlockSpec((None, d, tn), lambda l, j: (l, 0, j)),
            pl.BlockSpec((None, 1, tn), lambda l, j: (l, 0, j)),
        ],
        out_specs=pl.BlockSpec((None, rows, tn), lambda l, j: (l, 0, j)),
        out_shape=jax.ShapeDtypeStruct((depth, rows, n), F32),
        compiler_params=_params(("parallel", "parallel")),
        name="ada_mods",
    )(cc, ada_w, ada_b.reshape(depth, 1, n))


def _modulated(x, g, shift, scale):
    ms = jnp.mean(x * x, axis=-1, keepdims=True)
    return (x * lax.rsqrt(ms + EPS) * g) * (1.0 + scale) + shift


def _norm_kernel(x_ref, g_ref, sh_ref, sc_ref, o_ref, *, tm, n_ctx):
    is_ctx = _row_ids(pl.program_id(0), tm) < n_ctx
    h = _modulated(x_ref[...], g_ref[...], _sel_rows(is_ctx, sh_ref), _sel_rows(is_ctx, sc_ref))
    o_ref[...] = h.astype(o_ref.dtype)


def _norm_mod(z, g, mod, k_shift, n_ctx):
    m, d = z.shape
    tm = _pick(n_ctx, (256, 128, 64, 32, 16))
    return pl.pallas_call(
        functools.partial(_norm_kernel, tm=tm, n_ctx=n_ctx),
        grid=(m // tm,),
        in_specs=[
            pl.BlockSpec((tm, d), lambda i: (i, 0)),
            pl.BlockSpec((1, d), lambda i: (0, 0)),
            pl.BlockSpec((2, d), lambda i: (0, k_shift)),
            pl.BlockSpec((2, d), lambda i: (0, k_shift + 1)),
        ],
        out_specs=pl.BlockSpec((tm, d), lambda i: (i, 0)),
        out_shape=jax.ShapeDtypeStruct((m, d), BF16),
        compiler_params=_params(("parallel",)),
        name="norm_mod",
    )(z, g.reshape(1, d), mod, mod)


def _mm_tiles(m, n, n_ctx):
    tm = _pick(m, (768, 512, 256, 128))
    tn = _pick(n, (512, 256, 128))
    return tm, tn


def _resid_kernel(a_ref, w_ref, x_ref, g_ref, cs_ref, o_ref, *, tm, n_ctx):
    is_ctx = _row_ids(pl.program_id(0), tm) < n_ctx
    acc = _dot(a_ref[...], w_ref[...]) * cs_ref[...]
    o_ref[...] = x_ref[...] + _sel_rows(is_ctx, g_ref) * acc


def _mm_resid(a, w, x, mod, k_gate, n_ctx, colscale=None, groups=1, name="mm_resid"):
    m, ka = a.shape
    n = x.shape[1]
    kg = ka // groups
    ng = n // groups
    tm, tn = _mm_tiles(m, ng, n_ctx)
    jn = ng // tn
    if colscale is None:
        colscale = jnp.ones((n,), F32)
    if groups == 1:
        w_spec = pl.BlockSpec((kg, tn), lambda i, j: (0, j))
    else:
        w_spec = pl.BlockSpec((None, kg, tn), lambda i, j: (j // jn, 0, j % jn))
    return pl.pallas_call(
        functools.partial(_resid_kernel, tm=tm, n_ctx=n_ctx),
        grid=(m // tm, n // tn),
        in_specs=[
            pl.BlockSpec((tm, kg), lambda i, j: (i, j // jn)),
            w_spec,
            pl.BlockSpec((tm, tn), lambda i, j: (i, j)),
            pl.BlockSpec((2, tn), lambda i, j: (0, k_gate * (n // tn) + j)),
            pl.BlockSpec((1, tn), lambda i, j: (0, j)),
        ],
        out_specs=pl.BlockSpec((tm, tn), lambda i, j: (i, j)),
        out_shape=jax.ShapeDtypeStruct((m, n), F32),
        compiler_params=_params(("parallel", "parallel")),
        name=name,
    )(a, w, x, mod, colscale.reshape(1, n))


def _bias_kernel(a_ref, w_ref, b_ref, o_ref):
    o_ref[...] = _dot(a_ref[...], w_ref[...]) + b_ref[...]


def _mm_bias(a, w, b, name="mm_bias"):
    m, k = a.shape
    n = w.shape[1]
    tm, tn = _mm_tiles(m, n, 0)
    return pl.pallas_call(
        _bias_kernel,
        grid=(m // tm, n // tn),
        in_specs=[
            pl.BlockSpec((tm, k), lambda i, j: (i, 0)),
            pl.BlockSpec((k, tn), lambda i, j: (0, j)),
            pl.BlockSpec((1, tn), lambda i, j: (0, j)),
        ],
        out_specs=pl.BlockSpec((tm, tn), lambda i, j: (i, j)),
        out_shape=jax.ShapeDtypeStruct((m, n), F32),
        compiler_params=_params(("parallel", "parallel")),
        name=name,
    )(a, w, b.reshape(1, n))


def _plain_kernel(a_ref, w_ref, o_ref):
    o_ref[...] = _dot(a_ref[...], w_ref[...]).astype(o_ref.dtype)


def _mm_plain(a, w, name="mm_plain"):
    m, k = a.shape
    n = w.shape[1]
    tm, tn = _mm_tiles(m, n, 0)
    return pl.pallas_call(
        _plain_kernel,
        grid=(m // tm, n // tn),
        in_specs=[
            pl.BlockSpec((tm, k), lambda i, j: (i, 0)),
            pl.BlockSpec((k, tn), lambda i, j: (0, j)),
        ],
        out_specs=pl.BlockSpec((tm, tn), lambda i, j: (i, j)),
        out_shape=jax.ShapeDtypeStruct((m, n), BF16),
        compiler_params=_params(("parallel", "parallel")),
        name=name,
    )(a, w)


def _ffn_up_kernel(a_ref, ap_ref, an_ref, wg_ref, wv_ref, cwg_ref, cwv_ref, cbg_ref, cbv_ref,
                   o_ref, ext_ref, *, tm, n_ctx, m_total):
    i = pl.program_id(0)
    h = BF16_SUBLANES

    @pl.when(pl.program_id(1) == 0)
    def _():
        ext_ref[0:h, :] = ap_ref[...]
        ext_ref[h:h + tm, :] = a_ref[...]
        ext_ref[h + tm:h + tm + h, :] = an_ref[...]

    rows = _row_ids(i, tm)
    seq_first = (rows == 0) | (rows == n_ctx)
    seq_last = (rows == n_ctx - 1) | (rows == m_total - 1)
    a_ext = ext_ref[...]
    n_ext = tm + 2 * h

    def conv(w_ref, cw_ref, cb_ref):
        u = _dot(a_ext, w_ref[...])
        u_prev = pltpu.roll(u, 1, axis=0)[h:h + tm]
        u_next = pltpu.roll(u, n_ext - 1, axis=0)[h:h + tm]
        u_prev = jnp.where(seq_first, 0.0, u_prev)
        u_next = jnp.where(seq_last, 0.0, u_next)
        cw = cw_ref[...]
        return u_prev * cw[0:1] + u[h:h + tm] * cw[1:2] + u_next * cw[2:3] + cb_ref[...]

    gate = conv(wg_ref, cwg_ref, cbg_ref)
    val = conv(wv_ref, cwv_ref, cbv_ref)
    o_ref[...] = (_silu(gate) * val).astype(o_ref.dtype)


def _ffn_up(hn, w_up, conv_w, conv_b, n_ctx):
    m, d = hn.shape
    dff = w_up.shape[1] // 2
    tm, tn = _mm_tiles(m, dff, n_ctx)
    jn = dff // tn
    h = BF16_SUBLANES
    rb = tm // h
    last_rb = m // h - 1
    return pl.pallas_call(
        functools.partial(_ffn_up_kernel, tm=tm, n_ctx=n_ctx, m_total=m),
        grid=(m // tm, jn),
        in_specs=[
            pl.BlockSpec((tm, d), lambda i, j: (i, 0)),
            pl.BlockSpec((h, d), lambda i, j: (jnp.maximum(i * rb - 1, 0), 0)),
            pl.BlockSpec((h, d), lambda i, j: (jnp.minimum((i + 1) * rb, last_rb), 0)),
            pl.BlockSpec((d, tn), lambda i, j: (0, j)),
            pl.BlockSpec((d, tn), lambda i, j: (0, jn + j)),
            pl.BlockSpec((CONV_WIDTH, tn), lambda i, j: (0, j)),
            pl.BlockSpec((CONV_WIDTH, tn), lambda i, j: (0, jn + j)),
            pl.BlockSpec((1, tn), lambda i, j: (0, j)),
            pl.BlockSpec((1, tn), lambda i, j: (0, jn + j)),
        ],
        out_specs=pl.BlockSpec((tm, tn), lambda i, j: (i, j)),
        out_shape=jax.ShapeDtypeStruct((m, dff), BF16),
        scratch_shapes=[pltpu.VMEM((tm + 2 * h, d), BF16)],
        compiler_params=_params(("parallel", "arbitrary")),
        name="ffn_up",
    )(hn, hn, hn, w_up, w_up, conv_w, conv_w, conv_b.reshape(1, -1), conv_b.reshape(1, -1))


def _conv_ffn(z, norm_g, mod, w_up, conv_w, conv_b, w_down, n_ctx):
    hn = _norm_mod(z, norm_g, mod, 3, n_ctx)
    act = _ffn_up(hn, w_up, conv_w, conv_b, n_ctx)
    return _mm_resid(act, w_down, z, mod, 5, n_ctx, name="ffn_down")


def _pool_kernel(x_ref, xp_ref, xn_ref, g_ref, sh_ref, sc_ref, o_ref, s_ref, *, tm, n_ctx, m_total):
    i = pl.program_id(0)
    p = POOL_HALO
    d = x_ref.shape[1]
    cg = d // len(POOL_WINDOWS)
    rows = _row_ids(i, tm)
    is_ctx = rows < n_ctx
    g = g_ref[...]

    def halo(ref, first_row):
        ids = first_row + lax.broadcasted_iota(jnp.int32, (p, 1), 0)
        c = ids < n_ctx
        return _modulated(ref[...], g, _sel_rows(c, sh_ref), _sel_rows(c, sc_ref))

    start = i * tm
    stop = start + tm
    prev_ok = jnp.logical_and(start != 0, start != n_ctx)
    next_ok = jnp.logical_and(stop != n_ctx, stop != m_total)
    s_ref[0:p, :] = jnp.where(prev_ok, halo(xp_ref, start - p), 0.0)
    s_ref[p:p + tm, :] = _modulated(x_ref[...], g, _sel_rows(is_ctx, sh_ref), _sel_rows(is_ctx, sc_ref))
    s_ref[p + tm:p + tm + p, :] = jnp.where(next_ok, halo(xn_ref, stop), 0.0)

    t = jnp.where(is_ctx, rows, rows - n_ctx)
    t_len = jnp.where(is_ctx, n_ctx, m_total - n_ctx)
    for gi, w in enumerate(POOL_WINDOWS):
        cols = slice(gi * cg, (gi + 1) * cg)
        acc = s_ref[p - w // 2:p - w // 2 + tm, cols]
        for k in range(1 - w // 2, w - w // 2):
            acc = acc + s_ref[p + k:p + k + tm, cols]
        lo = jnp.maximum(t - w // 2, 0)
        hi = jnp.minimum(t + w - w // 2, t_len)
        cnt = (hi - lo).astype(F32)
        o_ref[:, cols] = (acc / cnt - s_ref[p:p + tm, cols]).astype(o_ref.dtype)


def _pool_pre(z, g, mod, n_ctx):
    m, d = z.shape
    tm = _pick(n_ctx, (256, 128, 64, 32, 16))
    p = POOL_HALO
    rb = tm // p
    last_rb = m // p - 1
    return pl.pallas_call(
        functools.partial(_pool_kernel, tm=tm, n_ctx=n_ctx, m_total=m),
        grid=(m // tm,),
        in_specs=[
            pl.BlockSpec((tm, d), lambda i: (i, 0)),
            pl.BlockSpec((p, d), lambda i: (jnp.maximum(i * rb - 1, 0), 0)),
            pl.BlockSpec((p, d), lambda i: (jnp.minimum((i + 1) * rb, last_rb), 0)),
            pl.BlockSpec((1, d), lambda i: (0, 0)),
            pl.BlockSpec((2, d), lambda i: (0, 0)),
            pl.BlockSpec((2, d), lambda i: (0, 1)),
        ],
        out_specs=pl.BlockSpec((tm, d), lambda i: (i, 0)),
        out_shape=jax.ShapeDtypeStruct((m, d), BF16),
        scratch_shapes=[pltpu.VMEM((tm + 2 * p, d), F32)],
        compiler_params=_params(("parallel",)),
        name="pool_pre",
    )(z, z, z, g.reshape(1, d), mod, mod)


def _attn_proj_kernel(a_ref, w_ref, g_ref, flag_ref, c_ref, s_ref, o_ref):
    acc = _dot(a_ref[...], w_ref[...])
    hd = ATTN_HEAD_DIM
    cos = c_ref[...]
    sin = s_ref[...]
    for hh in range(acc.shape[1] // hd):
        cols = slice(hh * hd, (hh + 1) * hd)
        xh = acc[:, cols]
        ms = jnp.mean(xh * xh, axis=-1, keepdims=True)
        y = xh * lax.rsqrt(ms + EPS) * g_ref[:, cols]
        y = y * cos + pltpu.roll(y, hd // 2, axis=1) * sin
        o_ref[:, cols] = jnp.where(flag_ref[:, cols] > 0.5, y, xh).astype(o_ref.dtype)


def _attn_proj(hn, w, gain, flag, cos, sin):
    m, d = hn.shape
    n = w.shape[1]
    tm, tn = _mm_tiles(m, n, 0)
    return pl.pallas_call(
        _attn_proj_kernel,
        grid=(m // tm, n // tn),
        in_specs=[
            pl.BlockSpec((tm, d), lambda i, j: (i, 0)),
            pl.BlockSpec((d, tn), lambda i, j: (0, j)),
            pl.BlockSpec((1, tn), lambda i, j: (0, j)),
            pl.BlockSpec((1, tn), lambda i, j: (0, j)),
            pl.BlockSpec((tm, ATTN_HEAD_DIM), lambda i, j: (i, 0)),
            pl.BlockSpec((tm, ATTN_HEAD_DIM), lambda i, j: (i, 0)),
        ],
        out_specs=pl.BlockSpec((tm, tn), lambda i, j: (i, j)),
        out_shape=jax.ShapeDtypeStruct((m, n), BF16),
        compiler_params=_params(("parallel", "parallel")),
        name="attn_proj",
    )(hn, w, gain.reshape(1, n), flag.reshape(1, n), cos, sin)


def _flash_kernel(q_ref, k_ref, v_ref, o_ref, qs_ref, m_ref, l_ref, acc_ref, *, tq, tk, n_kc_ctx, n_kc_all):
    i = pl.program_id(1)
    hd = ATTN_HEAD_DIM
    for gi in range(ATTN_GROUP):
        qs_ref[gi * tq:(gi + 1) * tq, :] = q_ref[:, gi * hd:(gi + 1) * hd]
    m_ref[...] = jnp.full(m_ref.shape, -jnp.inf, F32)
    l_ref[...] = jnp.zeros(l_ref.shape, F32)
    acc_ref[...] = jnp.zeros(acc_ref.shape, F32)
    q = qs_ref[...]

    def step(c, carry):
        start = pl.multiple_of(c * tk, tk)
        k = k_ref[pl.ds(start, tk), :]
        v = v_ref[pl.ds(start, tk), :]
        s = lax.dot_general(q, k, (((1,), (1,)), ((), ())), preferred_element_type=F32)
        m_old = m_ref[...]
        m_new = jnp.maximum(m_old, jnp.max(s, axis=-1, keepdims=True))
        p = jnp.exp(s - m_new)
        alpha = jnp.exp(m_old - m_new)
        l_ref[...] = alpha * l_ref[...] + jnp.sum(p, axis=-1, keepdims=True)
        acc_ref[...] = alpha * acc_ref[...] + _dot(p.astype(BF16), v)
        m_ref[...] = m_new
        return carry

    n_kc = jnp.where(i < n_kc_ctx, n_kc_ctx, n_kc_all)
    lax.fori_loop(0, n_kc, step, 0)
    out = acc_ref[...] / l_ref[...]
    for gi in range(ATTN_GROUP):
        o_ref[:, gi * hd:(gi + 1) * hd] = out[gi * tq:(gi + 1) * tq].astype(o_ref.dtype)


def _flash(qkv, n_ctx, n_q_heads):
    m = qkv.shape[0]
    hd = ATTN_HEAD_DIM
    kvh = n_q_heads // ATTN_GROUP
    tq = _pick(n_ctx, (256, 128, 64, 32, 16))
    tk = tq
    gw = ATTN_GROUP * hd
    return pl.pallas_call(
        functools.partial(_flash_kernel, tq=tq, tk=tk, n_kc_ctx=n_ctx // tk, n_kc_all=m // tk),
        grid=(kvh, m // tq),
        in_specs=[
            pl.BlockSpec((tq, gw), lambda g, i: (i, g)),
            pl.BlockSpec((m, hd), lambda g, i: (0, n_q_heads + g)),
            pl.BlockSpec((m, hd), lambda g, i: (0, n_q_heads + kvh + g)),
        ],
        out_specs=pl.BlockSpec((tq, gw), lambda g, i: (i, g)),
        out_shape=jax.ShapeDtypeStruct((m, n_q_heads * hd), BF16),
        scratch_shapes=[
            pltpu.VMEM((ATTN_GROUP * tq, hd), BF16),
            pltpu.VMEM((ATTN_GROUP * tq, 1), F32),
            pltpu.VMEM((ATTN_GROUP * tq, 1), F32),
            pltpu.VMEM((ATTN_GROUP * tq, hd), F32),
        ],
        compiler_params=_params(("parallel", "arbitrary")),
        name="flash_attn",
    )(qkv, qkv, qkv)


def _gate_prep_kernel(x_ref, o_ref, *, n_heads):
    x = x_ref[...]
    rows = x.shape[0]
    col = lax.broadcasted_iota(jnp.int32, x.shape, 1)
    is_f = jnp.logical_and((col // n_heads) % 2 == 1, col < 4 * n_heads)
    backward = col >= 2 * n_heads
    logf = _log_sigmoid(x)
    r = lax.broadcasted_iota(jnp.int32, (rows, rows), 0)
    c = lax.broadcasted_iota(jnp.int32, (rows, rows), 1)
    tril = (c <= r).astype(F32)
    triu = (c >= r).astype(F32)
    fwd = jnp.dot(tril, logf, preferred_element_type=F32, precision=lax.Precision.HIGHEST)
    bwd = jnp.dot(triu, logf, preferred_element_type=F32, precision=lax.Precision.HIGHEST)
    o_ref[...] = jnp.where(is_f, jnp.where(backward, bwd, fwd), x)


def _gate_prep(gates, n_heads):
    m, n = gates.shape
    ch = SCAN_CHUNK
    return pl.pallas_call(
        functools.partial(_gate_prep_kernel, n_heads=n_heads),
        grid=(m // ch,),
        in_specs=[pl.BlockSpec((ch, n), lambda c: (c, 0))],
        out_specs=pl.BlockSpec((ch, n), lambda c: (c, 0)),
        out_shape=jax.ShapeDtypeStruct((m, n), F32),
        compiler_params=_params(("parallel",)),
        name="gate_prep",
    )(gates)


def _scan_block(n_blk):
    def blk(d, c):
        return jnp.where(c == 0, 0, jnp.where(d == 0, c, n_blk - c))
    return blk


def _mlstm_kernel(q_ref, k_ref, v_ref, gc_ref, ir_ref, br_ref, o_ref, s_ref, n_ref, m_ref, *, n_heads):
    d = pl.program_id(0)
    hh = pl.program_id(1)
    ch = q_ref.shape[0]

    @pl.when(pl.program_id(2) == 0)
    def _():
        s_ref[...] = jnp.zeros(s_ref.shape, F32)
        n_ref[...] = jnp.zeros(n_ref.shape, F32)
        m_ref[...] = jnp.zeros(m_ref.shape, F32)

    q = q_ref[...]
    k = k_ref[...]
    v = v_ref[...]
    gc = gc_ref[...]
    lane = lax.broadcasted_iota(jnp.int32, gc.shape, 1)
    base = d * 2 * n_heads + hh
    i_col = jnp.sum(jnp.where(lane == base, gc, 0.0), axis=1, keepdims=True)
    b_col = jnp.sum(jnp.where(lane == base + n_heads, gc, 0.0), axis=1, keepdims=True)
    i_row = ir_ref[...]
    b_row = br_ref[...]
    m_prev = m_ref[...]

    r = lax.broadcasted_iota(jnp.int32, (ch, ch), 0)
    c = lax.broadcasted_iota(jnp.int32, (ch, ch), 1)
    sgn = 1 - 2 * d
    seen = (r - c) * sgn >= 0
    log_d = jnp.where(seen, b_col - b_row + i_row, -jnp.inf)
    log_prev = b_col + m_prev
    m_t = jnp.maximum(log_prev, jnp.max(log_d, axis=-1, keepdims=True))
    qk = lax.dot_general(q, k, (((1,), (1,)), ((), ())), preferred_element_type=F32)
    s = qk * jnp.exp(log_d - m_t)
    w_prev = jnp.exp(log_prev - m_t)
    qf = q.astype(F32)
    num = _dot(s.astype(BF16), v) + w_prev * _dot(q, s_ref[...].astype(BF16))
    den = jnp.sum(s, axis=-1, keepdims=True) + w_prev * jnp.sum(qf * n_ref[...], axis=-1, keepdims=True)
    o_ref[...] = num / jnp.maximum(jnp.abs(den), jnp.exp(-m_t))

    b_last = jnp.where(d == 0, b_row[:, ch - 1:ch], b_row[:, 0:1])
    log_w_row = b_last - b_row + i_row
    m_new = jnp.maximum(b_last + m_prev, jnp.max(log_w_row, axis=-1, keepdims=True))
    w_col = jnp.exp(b_last - b_col + i_col - m_new)
    decay = jnp.exp(b_last + m_prev - m_new)
    wv = (w_col * v.astype(F32)).astype(BF16)
    s_ref[...] = decay * s_ref[...] + lax.dot_general(k, wv, (((0,), (0,)), ((), ())),
                                                      preferred_element_type=F32)
    n_ref[...] = decay * n_ref[...] + jnp.sum(w_col * k.astype(F32), axis=0, keepdims=True)
    m_ref[...] = m_new


def _mlstm_scan(qkv, g2, g2t, n_heads, dk, dv):
    m = qkv.shape[0]
    ch = SCAN_CHUNK
    n_blk = m // ch
    blk = _scan_block(n_blk)
    kq = n_heads * dk // dk
    kv = 2 * n_heads * dk // dv
    return pl.pallas_call(
        functools.partial(_mlstm_kernel, n_heads=n_heads),
        grid=(2, n_heads, n_blk),
        in_specs=[
            pl.BlockSpec((ch, dk), lambda d, h, c: (blk(d, c), h)),
            pl.BlockSpec((ch, dk), lambda d, h, c: (blk(d, c), kq + h)),
            pl.BlockSpec((ch, dv), lambda d, h, c: (blk(d, c), kv + h)),
            pl.BlockSpec((ch, g2.shape[1]), lambda d, h, c: (blk(d, c), 0)),
            pl.BlockSpec((None, 1, ch), lambda d, h, c: (d * 2 * n_heads + h, 0, blk(d, c))),
            pl.BlockSpec((None, 1, ch), lambda d, h, c: (d * 2 * n_heads + n_heads + h, 0, blk(d, c))),
        ],
        out_specs=pl.BlockSpec((None, ch, dv), lambda d, h, c: (d, blk(d, c), h)),
        out_shape=jax.ShapeDtypeStruct((2, m, n_heads * dv), F32),
        scratch_shapes=[
            pltpu.VMEM((dk, dv), F32),
            pltpu.VMEM((1, dk), F32),
            pltpu.VMEM((1, 1), F32),
        ],
        compiler_params=_params(("parallel", "parallel", "arbitrary")),
        name="mlstm_scan",
    )(qkv, qkv, qkv, g2, g2t, g2t)


def _gated_norm_kernel(a_ref, w_ref, hf_ref, hb_ref, g_ref, o_ref, *, hw, act):
    acc = _dot(a_ref[...], w_ref[...])
    for hh in range(acc.shape[1] // hw):
        cols = slice(hh * hw, (hh + 1) * hw)
        y = hf_ref[:, cols] + hb_ref[:, cols]
        ms = jnp.mean(y * y, axis=-1, keepdims=True)
        y = y * lax.rsqrt(ms + EPS) * g_ref[:, cols]
        o_ref[:, cols] = (act(acc[:, cols]) * y).astype(o_ref.dtype)


def _mm_gated_norm(hn, w, hfb, gain, hw, act, name):
    m, d = hn.shape
    n = w.shape[1]
    tm, tn = _mm_tiles(m, n, 0)
    tn = max(tn, hw)
    return pl.pallas_call(
        functools.partial(_gated_norm_kernel, hw=hw, act=act),
        grid=(m // tm, n // tn),
        in_specs=[
            pl.BlockSpec((tm, d), lambda i, j: (i, 0)),
            pl.BlockSpec((d, tn), lambda i, j: (0, j)),
            pl.BlockSpec((None, tm, tn), lambda i, j: (0, i, j)),
            pl.BlockSpec((None, tm, tn), lambda i, j: (1, i, j)),
            pl.BlockSpec((1, tn), lambda i, j: (0, j)),
        ],
        out_specs=pl.BlockSpec((tm, tn), lambda i, j: (i, j)),
        out_shape=jax.ShapeDtypeStruct((m, n), BF16),
        compiler_params=_params(("parallel", "parallel")),
        name=name,
    )(hn, w, hfb, hfb, gain.reshape(1, n))


def _ret_proj_kernel(a_ref, w_ref, flag_ref, c_ref, s_ref, o_ref, *, hd):
    acc = _dot(a_ref[...], w_ref[...])
    cos = c_ref[...]
    sin = s_ref[...]
    half = hd // 2
    for hh in range(acc.shape[1] // hd):
        x0 = acc[:, hh * hd:hh * hd + half]
        x1 = acc[:, hh * hd + half:(hh + 1) * hd]
        rot = flag_ref[:, hh * hd:hh * hd + half] > 0.5
        o_ref[:, hh * hd:hh * hd + half] = jnp.where(rot, x0 * cos - x1 * sin, x0).astype(o_ref.dtype)
        o_ref[:, hh * hd + half:(hh + 1) * hd] = jnp.where(rot, x0 * sin + x1 * cos, x1).astype(o_ref.dtype)


def _ret_proj(hn, w, flag, cos, sin, hd):
    m, d = hn.shape
    n = w.shape[1]
    tm, tn = _mm_tiles(m, n, 0)
    tn = max(tn, hd)
    half = hd // 2
    return pl.pallas_call(
        functools.partial(_ret_proj_kernel, hd=hd),
        grid=(m // tm, n // tn),
        in_specs=[
            pl.BlockSpec((tm, d), lambda i, j: (i, 0)),
            pl.BlockSpec((d, tn), lambda i, j: (0, j)),
            pl.BlockSpec((1, tn), lambda i, j: (0, j)),
            pl.BlockSpec((tm, half), lambda i, j: (i, 0)),
            pl.BlockSpec((tm, half), lambda i, j: (i, 0)),
        ],
        out_specs=pl.BlockSpec((tm, tn), lambda i, j: (i, j)),
        out_shape=jax.ShapeDtypeStruct((m, n), BF16),
        compiler_params=_params(("parallel", "parallel")),
        name="ret_proj",
    )(hn, w, flag.reshape(1, n), cos, sin)


def _ret_kernel(q_ref, k_ref, v_ref, lg_ref, o_ref, s_ref):
    d = pl.program_id(0)
    ch = q_ref.shape[0]

    @pl.when(pl.program_id(2) == 0)
    def _():
        s_ref[...] = jnp.zeros(s_ref.shape, F32)

    q = q_ref[...]
    k = k_ref[...]
    v = v_ref[...]
    lg = _log_sigmoid(lg_ref[...])[:, 0:1]
    r = lax.broadcasted_iota(jnp.int32, (ch, ch), 0)
    c = lax.broadcasted_iota(jnp.int32, (ch, ch), 1)
    sgn = 1 - 2 * d
    rel = (r - c) * sgn
    seen = rel >= 0
    decay_in = jnp.where(seen, jnp.exp(lg * jnp.where(seen, rel, 0).astype(F32)), 0.0)
    t = lax.broadcasted_iota(jnp.int32, (ch, 1), 0)
    pos = jnp.where(d == 0, t, ch - 1 - t).astype(F32)
    xi = jnp.exp(lg * (pos + 1.0))
    zeta = jnp.exp(lg * (ch - 1.0 - pos))
    gamma_chunk = jnp.exp(lg * ch)
    qk = lax.dot_general(q, k, (((1,), (1,)), ((), ())), preferred_element_type=F32)
    sc = (qk * decay_in).astype(BF16)
    o_ref[...] = _dot(sc, v) + _dot(q, s_ref[...].astype(BF16)) * xi
    kz = (k.astype(F32) * zeta).astype(BF16)
    s_ref[...] = gamma_chunk * s_ref[...] + lax.dot_general(kz, v, (((0,), (0,)), ((), ())),
                                                            preferred_element_type=F32)


def _ret_scan(qkv, decay_logit, n_heads, hd):
    m = qkv.shape[0]
    ch = SCAN_CHUNK
    n_blk = m // ch
    blk = _scan_block(n_blk)
    lg = jnp.broadcast_to(decay_logit.astype(F32).reshape(2 * n_heads, 1, 1), (2 * n_heads, 1, LANES))
    return pl.pallas_call(
        _ret_kernel,
        grid=(2, n_heads, n_blk),
        in_specs=[
            pl.BlockSpec((ch, hd), lambda d, h, c: (blk(d, c), h)),
            pl.BlockSpec((ch, hd), lambda d, h, c: (blk(d, c), n_heads + h)),
            pl.BlockSpec((ch, hd), lambda d, h, c: (blk(d, c), 2 * n_heads + h)),
            pl.BlockSpec((None, 1, LANES), lambda d, h, c: (d * n_heads + h, 0, 0)),
        ],
        out_specs=pl.BlockSpec((None, ch, hd), lambda d, h, c: (d, blk(d, c), h)),
        out_shape=jax.ShapeDtypeStruct((2, m, n_heads * hd), F32),
        scratch_shapes=[pltpu.VMEM((hd, hd), F32)],
        compiler_params=_params(("parallel", "parallel", "arbitrary")),
        name="ret_scan",
    )(qkv, qkv, qkv, lg)


def _split_pairs(w, n_heads, hd):
    lead = w.shape[:-1]
    return w.reshape(*lead, n_heads, hd // 2, 2).swapaxes(-1, -2).reshape(*lead, n_heads * hd)


def _rope_tables(n_lat, n_ctx, head_dim):
    t = jnp.arange(n_lat)
    row = (t // GRID_W).astype(F32)
    col = (t % GRID_W).astype(F32)
    axis_dim = head_dim // 2
    inv_freq = 1.0 / (ROPE_THETA ** (jnp.arange(0, axis_dim, 2, dtype=F32) / axis_dim))
    ang = jnp.concatenate([row[:, None] * inv_freq, col[:, None] * inv_freq], axis=-1)
    cos = jnp.concatenate([jnp.ones((n_ctx, head_dim // 2), F32), jnp.cos(ang)], axis=0)
    sin = jnp.concatenate([jnp.zeros((n_ctx, head_dim // 2), F32), jnp.sin(ang)], axis=0)
    return cos, sin


def kernel(x, c, ctx, c_ctx, ada_w, ada_b, norm1_g, norm2_g, ffn_w_up, ffn_conv_w, ffn_conv_b, ffn_w_down, pool_w, pool_scale, attn_wq, attn_wk, attn_wv, attn_wo, attn_q_norm, attn_k_norm, mlstm_wq, mlstm_wk, mlstm_wv, mlstm_w_gates, mlstm_b_gates, mlstm_w_ogate, mlstm_out_norm, mlstm_wo, ret_wq, ret_wk, ret_wv, ret_wg, ret_decay_logit, ret_out_norm, ret_wo):
    batch, n_lat, d = x.shape
    n_ctx = ctx.shape[1]
    depth = ada_w.shape[0]
    assert batch == 1 and n_ctx % BF16_SUBLANES == 0 and n_lat % n_ctx == 0

    z = jnp.concatenate([ctx[0], x[0]], axis=0)
    cc = jnp.zeros((8, d), F32).at[0].set(c[0]).at[1].set(c_ctx)
    mods = _ada_mods(cc, ada_w, ada_b)

    for i in range(depth):
        kind, j = i % 4, i // 4
        mod = mods[i, :2]
        if kind == 0:
            pooled = _pool_pre(z, norm1_g[i], mod, n_ctx)
            z = _mm_resid(pooled, pool_w[j].astype(BF16), z, mod, 2, n_ctx, colscale=pool_scale[j],
                          groups=len(POOL_WINDOWS), name="pool_mix")
        elif kind == 1:
            hd = ATTN_HEAD_DIM
            qh = attn_wq.shape[2] // hd
            kvh = attn_wk.shape[2] // hd
            hn = _norm_mod(z, norm1_g[i], mod, 0, n_ctx)
            w = jnp.concatenate([_split_pairs(attn_wq[j], qh, hd), _split_pairs(attn_wk[j], kvh, hd),
                                 attn_wv[j]], axis=1).astype(BF16)
            gain = jnp.concatenate([jnp.tile(_split_pairs(attn_q_norm[j], 1, hd) * hd ** -0.5, qh),
                                    jnp.tile(_split_pairs(attn_k_norm[j], 1, hd), kvh),
                                    jnp.ones((kvh * hd,), F32)])
            flag = jnp.concatenate([jnp.ones(((qh + kvh) * hd,), F32), jnp.zeros((kvh * hd,), F32)])
            cos, sin = _rope_tables(n_lat, n_ctx, hd)
            cos2 = jnp.concatenate([cos, cos], axis=1)
            sin2 = jnp.concatenate([-sin, sin], axis=1)
            qkv = _attn_proj(hn, w, gain, flag, cos2, sin2)
            o = _flash(qkv, n_ctx, qh)
            z = _mm_resid(o, attn_wo[j].astype(BF16), z, mod, 2, n_ctx, name="attn_out")
        elif kind == 2:
            nh = MLSTM_HEADS
            dk = mlstm_wq.shape[2] // nh
            dv = mlstm_wv.shape[2] // nh
            hn = _norm_mod(z, norm1_g[i], mod, 0, n_ctx)
            w = jnp.concatenate([mlstm_wq[j], mlstm_wk[j] * dk ** -0.5, mlstm_wv[j]], axis=1).astype(BF16)
            qkv = _mm_plain(hn, w, name="mlstm_proj")
            wg = jnp.zeros((d, LANES), F32).at[:, :4 * nh].set(mlstm_w_gates[j]).astype(BF16)
            bg = jnp.zeros((LANES,), F32).at[:4 * nh].set(mlstm_b_gates[j])
            g2 = _gate_prep(_mm_bias(hn, wg, bg, name="mlstm_gates"), nh)
            g2t = g2.T.reshape(LANES, 1, -1)
            hfb = _mlstm_scan(qkv, g2, g2t, nh, dk, dv)
            gated = _mm_gated_norm(hn, mlstm_w_ogate[j].astype(BF16), hfb, mlstm_out_norm[j], dv,
                                   _sigmoid, "mlstm_ogate")
            z = _mm_resid(gated, mlstm_wo[j].astype(BF16), z, mod, 2, n_ctx, name="mlstm_out")
        else:
            nh = RET_HEADS
            hd = ret_wq.shape[2] // nh
            hn = _norm_mod(z, norm1_g[i], mod, 0, n_ctx)
            w = jnp.concatenate([_split_pairs(ret_wq[j], nh, hd), _split_pairs(ret_wk[j] * hd ** -0.5, nh, hd),
                                 ret_wv[j]], axis=1).astype(BF16)
            flag = jnp.concatenate([jnp.ones((2 * nh * hd,), F32), jnp.zeros((nh * hd,), F32)])
            cos, sin = _rope_tables(n_lat, n_ctx, hd)
            qkv = _ret_proj(hn, w, flag, cos, sin, hd)
            ofb = _ret_scan(qkv, ret_decay_logit[j], nh, hd)
            gated = _mm_gated_norm(hn, ret_wg[j].astype(BF16), ofb, ret_out_norm[j], hd, _silu, "ret_gate")
            z = _mm_resid(gated, ret_wo[j].astype(BF16), z, mod, 2, n_ctx, name="ret_out")
        z = _conv_ffn(z, norm2_g[i], mod, ffn_w_up[i].astype(BF16), ffn_conv_w[i], ffn_conv_b[i],
                      ffn_w_down[i].astype(BF16), n_ctx)
    return z[n_ctx:][None]
```

```python
import functools

import jax
import jax.numpy as jnp
from jax import lax
from jax.experimental import pallas as pl
from jax.experimental.pallas import tpu as pltpu

F32 = jnp.float32
BF16 = jnp.bfloat16

EPS = 1e-6
GRID_W = 64
ROPE_THETA = 10000.0
N_MOD = 6
POOL_WINDOWS = (2, 4, 8, 16)
ATTN_HEAD_DIM = 128
ATTN_GROUP = 4
MLSTM_HEADS = 8
RET_HEADS = 16
CONV_WIDTH = 3
LOG2_E = 1.4426950408889634

VMEM_LIMIT_BYTES = 56 * 1024 * 1024
BF16_SUBLANES = 16
F32_SUBLANES = 8
LANES = 128
MXU_COLS = 256
SCAN_CHUNK = 256
SCAN_HEADS_PER_STEP = 2
NORM_UNROLL = 4
POOL_HALO = 8


def _pick(n, candidates):
    for c in candidates:
        if c <= n and n % c == 0:
            return c
    return n


def _params(sem):
    return pltpu.CompilerParams(dimension_semantics=sem, vmem_limit_bytes=VMEM_LIMIT_BYTES)


def _silu(x):
    return x * (1.0 / (1.0 + jnp.exp(-x)))


def _sigmoid(x):
    return 1.0 / (1.0 + jnp.exp(-x))


def _log_sigmoid(x):
    return jnp.minimum(x, 0.0) - jnp.log(1.0 + jnp.exp(-jnp.abs(x)))


def _row_ids(i, tm):
    return i * tm + lax.broadcasted_iota(jnp.int32, (tm, 1), 0)


def _sel_rows(is_ctx, ref2):
    return jnp.where(is_ctx, ref2[1:2, :], ref2[0:1, :])


def _dot(a, b):
    return jnp.dot(a, b, preferred_element_type=F32)


def _ada_kernel(a_ref, w_ref, b_ref, o_ref):
    a = _silu(a_ref[...]).astype(BF16)
    o_ref[...] = _dot(a, w_ref[...].astype(BF16)) + b_ref[...]


def _ada_mods(cc, ada_w, ada_b):
    depth, d, n = ada_w.shape
    tn = _pick(n, (512, 256, 128))
    rows = cc.shape[0]
    return pl.pallas_call(
        _ada_kernel,
        grid=(depth, n // tn),
        in_specs=[
            pl.BlockSpec((rows, d), lambda l, j: (0, 0)),
            pl.BlockSpec((None, d, tn), lambda l, j: (l, 0, j)),
            pl.BlockSpec((None, 1, tn), lambda l, j: (l, 0, j)),
        ],
        out_specs=pl.BlockSpec((None, rows, tn), lambda l, j: (l, 0, j)),
        out_shape=jax.ShapeDtypeStruct((depth, rows, n), F32),
        compiler_params=_params(("parallel", "parallel")),
        name="ada_mods",
    )(cc, ada_w, ada_b.reshape(depth, 1, n))


def _modulated(x, g, shift, scale):
    ms = jnp.mean(x * x, axis=-1, keepdims=True)
    return (x * lax.rsqrt(ms + EPS) * g) * (1.0 + scale) + shift


def _norm_kernel(x_ref, g_ref, sh_ref, sc_ref, o_ref, a_ref):
    a_ref[...] = g_ref[...] * (1.0 + sc_ref[...])
    s8 = F32_SUBLANES

    def body(r, carry):
        base = pl.multiple_of(r * BF16_SUBLANES, BF16_SUBLANES)
        ys = []
        for part in range(BF16_SUBLANES // s8):
            x = x_ref[pl.ds(pl.multiple_of(base + part * s8, s8), s8), :]
            ms = jnp.mean(x * x, axis=-1, keepdims=True)
            ys.append(x * lax.rsqrt(ms + EPS) * a_ref[...] + sh_ref[...])
        o_ref[pl.ds(base, BF16_SUBLANES), :] = jnp.concatenate(ys, axis=0).astype(o_ref.dtype)
        return carry

    lax.fori_loop(0, x_ref.shape[0] // BF16_SUBLANES, body, 0, unroll=NORM_UNROLL)


def _norm_mod(z, g, mod, k_shift, n_ctx):
    m, d = z.shape
    s8 = F32_SUBLANES
    tm = _pick(n_ctx, (256, 128, 64, 32, 16))
    mod8 = jnp.repeat(mod, s8, axis=0)
    vec_blk = lambda i: jnp.where(i * tm < n_ctx, 1, 0)
    return pl.pallas_call(
        _norm_kernel,
        grid=(m // tm,),
        in_specs=[
            pl.BlockSpec((tm, d), lambda i: (i, 0)),
            pl.BlockSpec((s8, d), lambda i: (0, 0)),
            pl.BlockSpec((s8, d), lambda i: (vec_blk(i), k_shift)),
            pl.BlockSpec((s8, d), lambda i: (vec_blk(i), k_shift + 1)),
        ],
        out_specs=pl.BlockSpec((tm, d), lambda i: (i, 0)),
        out_shape=jax.ShapeDtypeStruct((m, d), BF16),
        scratch_shapes=[pltpu.VMEM((s8, d), F32)],
        compiler_params=_params(("parallel",)),
        name="norm_mod",
    )(z, jnp.broadcast_to(g.reshape(1, d), (s8, d)), mod8, mod8)


def _mm_tiles(m, n, n_ctx):
    tm = _pick(m, (768, 512, 256, 128))
    tn = _pick(n, (512, 256, 128))
    return tm, tn


def _resid_kernel(a_ref, w_ref, x_ref, g_ref, cs_ref, o_ref, *, tm, n_ctx):
    is_ctx = _row_ids(pl.program_id(0), tm) < n_ctx
    acc = _dot(a_ref[...], w_ref[...]) * cs_ref[...]
    o_ref[...] = x_ref[...] + _sel_rows(is_ctx, g_ref) * acc


def _mm_resid(a, w, x, mod, k_gate, n_ctx, colscale=None, groups=1, name="mm_resid"):
    m, ka = a.shape
    n = x.shape[1]
    kg = ka // groups
    ng = n // groups
    tm, tn = _mm_tiles(m, ng, n_ctx)
    jn = ng // tn
    if colscale is None:
        colscale = jnp.ones((n,), F32)
    if groups == 1:
        w_spec = pl.BlockSpec((kg, tn), lambda i, j: (0, j))
    else:
        w_spec = pl.BlockSpec((None, kg, tn), lambda i, j: (j // jn, 0, j % jn))
    return pl.pallas_call(
        functools.partial(_resid_kernel, tm=tm, n_ctx=n_ctx),
        grid=(m // tm, n // tn),
        in_specs=[
            pl.BlockSpec((tm, kg), lambda i, j: (i, j // jn)),
            w_spec,
            pl.BlockSpec((tm, tn), lambda i, j: (i, j)),
            pl.BlockSpec((2, tn), lambda i, j: (0, k_gate * (n // tn) + j)),
            pl.BlockSpec((1, tn), lambda i, j: (0, j)),
        ],
        out_specs=pl.BlockSpec((tm, tn), lambda i, j: (i, j)),
        out_shape=jax.ShapeDtypeStruct((m, n), F32),
        compiler_params=_params(("parallel", "parallel")),
        name=name,
    )(a, w, x, mod, colscale.reshape(1, n))


def _bias_kernel(a_ref, w_ref, b_ref, o_ref):
    o_ref[...] = _dot(a_ref[...], w_ref[...]) + b_ref[...]


def _mm_bias(a, w, b, name="mm_bias"):
    m, k = a.shape
    n = w.shape[1]
    tm, tn = _mm_tiles(m, n, 0)
    return pl.pallas_call(
        _bias_kernel,
        grid=(m // tm, n // tn),
        in_specs=[
            pl.BlockSpec((tm, k), lambda i, j: (i, 0)),
            pl.BlockSpec((k, tn), lambda i, j: (0, j)),
            pl.BlockSpec((1, tn), lambda i, j: (0, j)),
        ],
        out_specs=pl.BlockSpec((tm, tn), lambda i, j: (i, j)),
        out_shape=jax.ShapeDtypeStruct((m, n), F32),
        compiler_params=_params(("parallel", "parallel")),
        name=name,
    )(a, w, b.reshape(1, n))


def _plain_kernel(a_ref, w_ref, o_ref):
    o_ref[...] = _dot(a_ref[...], w_ref[...]).astype(o_ref.dtype)


def _mm_plain(a, w, name="mm_plain"):
    m, k = a.shape
    n = w.shape[1]
    tm, tn = _mm_tiles(m, n, 0)
    return pl.pallas_call(
        _plain_kernel,
        grid=(m // tm, n // tn),
        in_specs=[
            pl.BlockSpec((tm, k), lambda i, j: (i, 0)),
            pl.BlockSpec((k, tn), lambda i, j: (0, j)),
        ],
        out_specs=pl.BlockSpec((tm, tn), lambda i, j: (i, j)),
        out_shape=jax.ShapeDtypeStruct((m, n), BF16),
        compiler_params=_params(("parallel", "parallel")),
        name=name,
    )(a, w)


def _ffn_up_kernel(a_ref, ap_ref, an_ref, wg_ref, wv_ref, cwg_ref, cwv_ref, cbg_ref, cbv_ref,
                   o_ref, ext_ref, *, tm, n_ctx, m_total):
    i = pl.program_id(0)
    h = BF16_SUBLANES

    @pl.when(pl.program_id(1) == 0)
    def _():
        ext_ref[0:h, :] = ap_ref[...]
        ext_ref[h:h + tm, :] = a_ref[...]
        ext_ref[h + tm:h + tm + h, :] = an_ref[...]

    rows = _row_ids(i, tm)
    seq_first = (rows == 0) | (rows == n_ctx)
    seq_last = (rows == n_ctx - 1) | (rows == m_total - 1)
    a_ext = ext_ref[...]
    n_ext = tm + 2 * h

    def conv(w_ref, cw_ref, cb_ref, cols):
        u = _dot(a_ext, w_ref[:, cols])
        u_prev = pltpu.roll(u, 1, axis=0)[h:h + tm]
        u_next = pltpu.roll(u, n_ext - 1, axis=0)[h:h + tm]
        u_prev = jnp.where(seq_first, 0.0, u_prev)
        u_next = jnp.where(seq_last, 0.0, u_next)
        cw = cw_ref[:, cols]
        return u_prev * cw[0:1] + u[h:h + tm] * cw[1:2] + u_next * cw[2:3] + cb_ref[:, cols]

    cols = slice(0, o_ref.shape[1])
    gate = conv(wg_ref, cwg_ref, cbg_ref, cols)
    val = conv(wv_ref, cwv_ref, cbv_ref, cols)
    o_ref[...] = (_silu(gate) * val).astype(o_ref.dtype)


def _ffn_up(hn, w_up, conv_w, conv_b, n_ctx):
    m, d = hn.shape
    dff = w_up.shape[1] // 2
    tm, tn = _mm_tiles(m, dff, n_ctx)
    jn = dff // tn
    h = BF16_SUBLANES
    rb = tm // h
    last_rb = m // h - 1
    return pl.pallas_call(
        functools.partial(_ffn_up_kernel, tm=tm, n_ctx=n_ctx, m_total=m),
        grid=(m // tm, jn),
        in_specs=[
            pl.BlockSpec((tm, d), lambda i, j: (i, 0)),
            pl.BlockSpec((h, d), lambda i, j: (jnp.maximum(i * rb - 1, 0), 0)),
            pl.BlockSpec((h, d), lambda i, j: (jnp.minimum((i + 1) * rb, last_rb), 0)),
            pl.BlockSpec((d, tn), lambda i, j: (0, j)),
            pl.BlockSpec((d, tn), lambda i, j: (0, jn + j)),
            pl.BlockSpec((CONV_WIDTH, tn), lambda i, j: (0, j)),
            pl.BlockSpec((CONV_WIDTH, tn), lambda i, j: (0, jn + j)),
            pl.BlockSpec((1, tn), lambda i, j: (0, j)),
            pl.BlockSpec((1, tn), lambda i, j: (0, jn + j)),
        ],
        out_specs=pl.BlockSpec((tm, tn), lambda i, j: (i, j)),
        out_shape=jax.ShapeDtypeStruct((m, dff), BF16),
        scratch_shapes=[pltpu.VMEM((tm + 2 * h, d), BF16)],
        compiler_params=_params(("parallel", "arbitrary")),
        name="ffn_up",
    )(hn, hn, hn, w_up, w_up, conv_w, conv_w, conv_b.reshape(1, -1), conv_b.reshape(1, -1))


def _conv_ffn(z, norm_g, mod, w_up, conv_w, conv_b, w_down, n_ctx):
    hn = _norm_mod(z, norm_g, mod, 3, n_ctx)
    act = _ffn_up(hn, w_up, conv_w, conv_b, n_ctx)
    return _mm_resid(act, w_down, z, mod, 5, n_ctx, name="ffn_down")


def _pool_kernel(x_ref, xp_ref, xn_ref, g_ref, sh_ref, sc_ref, o_ref, s_ref, *, tm, n_ctx, m_total):
    i = pl.program_id(0)
    p = POOL_HALO
    d = x_ref.shape[1]
    cg = d // len(POOL_WINDOWS)
    rows = _row_ids(i, tm)
    is_ctx = rows < n_ctx
    g = g_ref[...]

    def halo(ref, first_row):
        ids = first_row + lax.broadcasted_iota(jnp.int32, (p, 1), 0)
        c = ids < n_ctx
        return _modulated(ref[...], g, _sel_rows(c, sh_ref), _sel_rows(c, sc_ref))

    start = i * tm
    stop = start + tm
    prev_ok = jnp.logical_and(start != 0, start != n_ctx)
    next_ok = jnp.logical_and(stop != n_ctx, stop != m_total)
    s_ref[0:p, :] = jnp.where(prev_ok, halo(xp_ref, start - p), 0.0)
    s_ref[p:p + tm, :] = _modulated(x_ref[...], g, _sel_rows(is_ctx, sh_ref), _sel_rows(is_ctx, sc_ref))
    s_ref[p + tm:p + tm + p, :] = jnp.where(next_ok, halo(xn_ref, stop), 0.0)

    t = jnp.where(is_ctx, rows, rows - n_ctx)
    t_len = jnp.where(is_ctx, n_ctx, m_total - n_ctx)
    for gi, w in enumerate(POOL_WINDOWS):
        cols = slice(gi * cg, (gi + 1) * cg)
        acc = s_ref[p - w // 2:p - w // 2 + tm, cols]
        for k in range(1 - w // 2, w - w // 2):
            acc = acc + s_ref[p + k:p + k + tm, cols]
        lo = jnp.maximum(t - w // 2, 0)
        hi = jnp.minimum(t + w - w // 2, t_len)
        cnt = (hi - lo).astype(F32)
        o_ref[:, cols] = (acc / cnt - s_ref[p:p + tm, cols]).astype(o_ref.dtype)


def _pool_pre(z, g, mod, n_ctx):
    m, d = z.shape
    tm = _pick(n_ctx, (256, 128, 64, 32, 16))
    p = POOL_HALO
    rb = tm // p
    last_rb = m // p - 1
    return pl.pallas_call(
        functools.partial(_pool_kernel, tm=tm, n_ctx=n_ctx, m_total=m),
        grid=(m // tm,),
        in_specs=[
            pl.BlockSpec((tm, d), lambda i: (i, 0)),
            pl.BlockSpec((p, d), lambda i: (jnp.maximum(i * rb - 1, 0), 0)),
            pl.BlockSpec((p, d), lambda i: (jnp.minimum((i + 1) * rb, last_rb), 0)),
            pl.BlockSpec((1, d), lambda i: (0, 0)),
            pl.BlockSpec((2, d), lambda i: (0, 0)),
            pl.BlockSpec((2, d), lambda i: (0, 1)),
        ],
        out_specs=pl.BlockSpec((tm, d), lambda i: (i, 0)),
        out_shape=jax.ShapeDtypeStruct((m, d), BF16),
        scratch_shapes=[pltpu.VMEM((tm + 2 * p, d), F32)],
        compiler_params=_params(("parallel",)),
        name="pool_pre",
    )(z, z, z, g.reshape(1, d), mod, mod)


def _attn_proj_kernel(a_ref, w_ref, g_ref, flag_ref, c_ref, s_ref, o_ref):
    hd = ATTN_HEAD_DIM
    a = a_ref[...]
    cos = c_ref[...]
    sin = s_ref[...]
    tn = o_ref.shape[1]
    sub = min(tn, MXU_COLS)
    for c0 in range(0, tn, sub):
        acc = _dot(a, w_ref[:, c0:c0 + sub])
        for hh in range(sub // hd):
            cols = slice(c0 + hh * hd, c0 + (hh + 1) * hd)
            xh = acc[:, hh * hd:(hh + 1) * hd]
            ms = jnp.mean(xh * xh, axis=-1, keepdims=True)
            y = xh * lax.rsqrt(ms + EPS) * g_ref[:, cols]
            y = y * cos + pltpu.roll(y, hd // 2, axis=1) * sin
            o_ref[:, cols] = jnp.where(flag_ref[:, cols] > 0.5, y, xh).astype(o_ref.dtype)


def _attn_proj(hn, w, gain, flag, cos, sin):
    m, d = hn.shape
    n = w.shape[1]
    tm, tn = _mm_tiles(m, n, 0)
    return pl.pallas_call(
        _attn_proj_kernel,
        grid=(m // tm, n // tn),
        in_specs=[
            pl.BlockSpec((tm, d), lambda i, j: (i, 0)),
            pl.BlockSpec((d, tn), lambda i, j: (0, j)),
            pl.BlockSpec((1, tn), lambda i, j: (0, j)),
            pl.BlockSpec((1, tn), lambda i, j: (0, j)),
            pl.BlockSpec((tm, ATTN_HEAD_DIM), lambda i, j: (i, 0)),
            pl.BlockSpec((tm, ATTN_HEAD_DIM), lambda i, j: (i, 0)),
        ],
        out_specs=pl.BlockSpec((tm, tn), lambda i, j: (i, j)),
        out_shape=jax.ShapeDtypeStruct((m, n), BF16),
        compiler_params=_params(("parallel", "parallel")),
        name="attn_proj",
    )(hn, w, gain.reshape(1, n), flag.reshape(1, n), cos, sin)


def _flash_kernel(q_ref, kt_ref, v_ref, *rest, tq, chunk):
    o_ref, qs_ref, vx_ref = rest[-3:]
    hd = ATTN_HEAD_DIM
    n_keys = v_ref.shape[0]

    @pl.when(pl.program_id(1) == 0)
    def _():
        vx_ref[:, 0:hd] = v_ref[...]
        lane = lax.broadcasted_iota(jnp.int32, (n_keys, hd), 1)
        vx_ref[:, hd:2 * hd] = jnp.where(lane == 0, 1.0, 0.0).astype(BF16)

    for gi in range(ATTN_GROUP):
        qs_ref[gi * tq:(gi + 1) * tq, :] = q_ref[:, gi * hd:(gi + 1) * hd]
    q = qs_ref[...]
    m_run = None
    acc = None
    for start in range(0, n_keys, chunk):
        s = _dot(q, kt_ref[:, start:start + chunk])
        m_c = jnp.max(s, axis=-1, keepdims=True)
        m_new = m_c if m_run is None else jnp.maximum(m_run, m_c)
        p = jnp.exp2(s - m_new).astype(BF16)
        pv = _dot(p, vx_ref[start:start + chunk, :])
        acc = pv if acc is None else jnp.exp2(m_run - m_new) * acc + pv
        m_run = m_new
    out = acc[:, 0:hd] / acc[:, hd:hd + 1]
    for gi in range(ATTN_GROUP):
        o_ref[:, gi * hd:(gi + 1) * hd] = out[gi * tq:(gi + 1) * tq].astype(o_ref.dtype)


def _flash_part(qkv, kt, o_prev, row0, n_rows, n_keys, n_q_heads):
    m = qkv.shape[0]
    hd = ATTN_HEAD_DIM
    kvh = n_q_heads // ATTN_GROUP
    tq = _pick(n_rows, (256, 128, 64, 32, 16))
    chunk = _pick(n_keys, (1408, 1024, 768, 512, 256, 128))
    gw = ATTN_GROUP * hd
    r0 = row0 // tq
    in_specs = [
        pl.BlockSpec((tq, gw), lambda g, i: (r0 + i, g)),
        pl.BlockSpec((hd, n_keys), lambda g, i: (g, 0)),
        pl.BlockSpec((n_keys, hd), lambda g, i: (0, n_q_heads + kvh + g)),
    ]
    args = [qkv, kt, qkv]
    aliases = {}
    if o_prev is not None:
        in_specs.append(pl.BlockSpec(memory_space=pl.ANY))
        args.append(o_prev)
        aliases = {3: 0}
    return pl.pallas_call(
        functools.partial(_flash_kernel, tq=tq, chunk=chunk),
        grid=(kvh, n_rows // tq),
        in_specs=in_specs,
        out_specs=pl.BlockSpec((tq, gw), lambda g, i: (r0 + i, g)),
        out_shape=jax.ShapeDtypeStruct((m, n_q_heads * hd), BF16),
        scratch_shapes=[
            pltpu.VMEM((ATTN_GROUP * tq, hd), BF16),
            pltpu.VMEM((n_keys, 2 * hd), BF16),
        ],
        input_output_aliases=aliases,
        compiler_params=_params(("parallel", "arbitrary")),
        name="flash_attn",
    )(*args)


def _flash(qkv, n_ctx, n_q_heads):
    m = qkv.shape[0]
    hd = ATTN_HEAD_DIM
    kvh = n_q_heads // ATTN_GROUP
    kt = qkv[:, n_q_heads * hd:(n_q_heads + kvh) * hd].T
    o = _flash_part(qkv, kt, None, n_ctx, m - n_ctx, m, n_q_heads)
    return _flash_part(qkv, kt, o, 0, n_ctx, n_ctx, n_q_heads)


def _gate_prep_kernel(x_ref, o_ref, *, n_heads):
    x = x_ref[...]
    rows = x.shape[0]
    col = lax.broadcasted_iota(jnp.int32, x.shape, 1)
    is_f = jnp.logical_and((col // n_heads) % 2 == 1, col < 4 * n_heads)
    backward = col >= 2 * n_heads
    logf = _log_sigmoid(x)
    r = lax.broadcasted_iota(jnp.int32, (rows, rows), 0)
    c = lax.broadcasted_iota(jnp.int32, (rows, rows), 1)
    tril = (c <= r).astype(F32)
    triu = (c >= r).astype(F32)
    fwd = jnp.dot(tril, logf, preferred_element_type=F32, precision=lax.Precision.HIGHEST)
    bwd = jnp.dot(triu, logf, preferred_element_type=F32, precision=lax.Precision.HIGHEST)
    o_ref[...] = jnp.where(is_f, jnp.where(backward, bwd, fwd), x)


def _gate_prep(gates, n_heads):
    m, n = gates.shape
    ch = SCAN_CHUNK
    return pl.pallas_call(
        functools.partial(_gate_prep_kernel, n_heads=n_heads),
        grid=(m // ch,),
        in_specs=[pl.BlockSpec((ch, n), lambda c: (c, 0))],
        out_specs=pl.BlockSpec((ch, n), lambda c: (c, 0)),
        out_shape=jax.ShapeDtypeStruct((m, n), F32),
        compiler_params=_params(("parallel",)),
        name="gate_prep",
    )(gates)


def _scan_block(n_blk):
    def blk(d, c):
        return jnp.where(c == 0, 0, jnp.where(d == 0, c, n_blk - c))
    return blk


def _mlstm_kernel(q_ref, k_ref, v_ref, gc_ref, ir_ref, br_ref, o_ref, s_ref, n_ref, m_ref, *, n_heads, hp):
    d = pl.program_id(0)
    ch = q_ref.shape[0]
    dk = q_ref.shape[1] // hp
    dv = v_ref.shape[1] // hp

    @pl.when(pl.program_id(2) == 0)
    def _():
        s_ref[...] = jnp.zeros(s_ref.shape, F32)
        n_ref[...] = jnp.zeros(n_ref.shape, F32)
        m_ref[...] = jnp.zeros(m_ref.shape, F32)

    gc = gc_ref[...]
    lane = lax.broadcasted_iota(jnp.int32, gc.shape, 1)
    r = lax.broadcasted_iota(jnp.int32, (ch, ch), 0)
    c = lax.broadcasted_iota(jnp.int32, (ch, ch), 1)
    sgn = 1 - 2 * d
    seen = (r - c) * sgn >= 0

    for hl in range(hp):
        q = q_ref[:, hl * dk:(hl + 1) * dk]
        k = k_ref[:, hl * dk:(hl + 1) * dk]
        v = v_ref[:, hl * dv:(hl + 1) * dv]
        base = d * 2 * n_heads + pl.program_id(1) * hp + hl
        i_col = jnp.sum(jnp.where(lane == base, gc, 0.0), axis=1, keepdims=True)
        b_col = jnp.sum(jnp.where(lane == base + n_heads, gc, 0.0), axis=1, keepdims=True)
        i_row = ir_ref[hl]
        b_row = br_ref[hl]
        m_prev = m_ref[hl]
        s_prev = s_ref[hl]
        n_prev = n_ref[hl]

        log_d = jnp.where(seen, b_col - b_row + i_row, -jnp.inf)
        log_prev = b_col + m_prev
        m_t = jnp.maximum(log_prev, jnp.max(log_d, axis=-1, keepdims=True))
        qk = lax.dot_general(q, k, (((1,), (1,)), ((), ())), preferred_element_type=F32)
        s = qk * jnp.exp(log_d - m_t)
        w_prev = jnp.exp(log_prev - m_t)
        num = _dot(s.astype(BF16), v) + w_prev * _dot(q, s_prev.astype(BF16))
        den = (jnp.sum(s, axis=-1, keepdims=True)
               + w_prev * jnp.sum(q.astype(F32) * n_prev, axis=-1, keepdims=True))
        o_ref[:, hl * dv:(hl + 1) * dv] = num / jnp.maximum(jnp.abs(den), jnp.exp(-m_t))

        b_last = jnp.where(d == 0, b_row[:, ch - 1:ch], b_row[:, 0:1])
        log_w_row = b_last - b_row + i_row
        m_new = jnp.maximum(b_last + m_prev, jnp.max(log_w_row, axis=-1, keepdims=True))
        w_col = jnp.exp(b_last - b_col + i_col - m_new)
        decay = jnp.exp(b_last + m_prev - m_new)
        wv = (w_col * v.astype(F32)).astype(BF16)
        s_ref[hl] = decay * s_prev + lax.dot_general(k, wv, (((0,), (0,)), ((), ())),
                                                     preferred_element_type=F32)
        n_ref[hl] = decay * n_prev + jnp.sum(w_col * k.astype(F32), axis=0, keepdims=True)
        m_ref[hl] = m_new


def _mlstm_scan(qkv, g2, g2t, n_heads, dk, dv):
    m = qkv.shape[0]
    ch = SCAN_CHUNK
    n_blk = m // ch
    blk = _scan_block(n_blk)
    hp = SCAN_HEADS_PER_STEP
    hb = n_heads // hp
    return pl.pallas_call(
        functools.partial(_mlstm_kernel, n_heads=n_heads, hp=hp),
        grid=(2, hb, n_blk),
        in_specs=[
            pl.BlockSpec((ch, hp * dk), lambda d, h, c: (blk(d, c), h)),
            pl.BlockSpec((ch, hp * dk), lambda d, h, c: (blk(d, c), hb + h)),
            pl.BlockSpec((ch, hp * dv), lambda d, h, c: (blk(d, c), 2 * n_heads * dk // (hp * dv) + h)),
            pl.BlockSpec((ch, g2.shape[1]), lambda d, h, c: (blk(d, c), 0)),
            pl.BlockSpec((hp, 1, ch), lambda d, h, c: (d * 2 * hb + h, 0, blk(d, c))),
            pl.BlockSpec((hp, 1, ch), lambda d, h, c: (d * 2 * hb + hb + h, 0, blk(d, c))),
        ],
        out_specs=pl.BlockSpec((None, ch, hp * dv), lambda d, h, c: (d, blk(d, c), h)),
        out_shape=jax.ShapeDtypeStruct((2, m, n_heads * dv), F32),
        scratch_shapes=[
            pltpu.VMEM((hp, dk, dv), F32),
            pltpu.VMEM((hp, 1, dk), F32),
            pltpu.VMEM((hp, 1, 1), F32),
        ],
        compiler_params=_params(("parallel", "parallel", "arbitrary")),
        name="mlstm_scan",
    )(qkv, qkv, qkv, g2, g2t, g2t)


def _gated_norm_kernel(a_ref, w_ref, hf_ref, hb_ref, g_ref, o_ref, *, hw, act):
    acc = _dot(a_ref[...], w_ref[...])
    for hh in range(acc.shape[1] // hw):
        cols = slice(hh * hw, (hh + 1) * hw)
        y = hf_ref[:, cols] + hb_ref[:, cols]
        ms = jnp.mean(y * y, axis=-1, keepdims=True)
        y = y * lax.rsqrt(ms + EPS) * g_ref[:, cols]
        o_ref[:, cols] = (act(acc[:, cols]) * y).astype(o_ref.dtype)


def _mm_gated_norm(hn, w, hfb, gain, hw, act, name):
    m, d = hn.shape
    n = w.shape[1]
    tm, tn = _mm_tiles(m, n, 0)
    tn = max(tn, hw)
    return pl.pallas_call(
        functools.partial(_gated_norm_kernel, hw=hw, act=act),
        grid=(m // tm, n // tn),
        in_specs=[
            pl.BlockSpec((tm, d), lambda i, j: (i, 0)),
            pl.BlockSpec((d, tn), lambda i, j: (0, j)),
            pl.BlockSpec((None, tm, tn), lambda i, j: (0, i, j)),
            pl.BlockSpec((None, tm, tn), lambda i, j: (1, i, j)),
            pl.BlockSpec((1, tn), lambda i, j: (0, j)),
        ],
        out_specs=pl.BlockSpec((tm, tn), lambda i, j: (i, j)),
        out_shape=jax.ShapeDtypeStruct((m, n), BF16),
        compiler_params=_params(("parallel", "parallel")),
        name=name,
    )(hn, w, hfb, hfb, gain.reshape(1, n))


def _ret_proj_kernel(a_ref, w_ref, flag_ref, c_ref, s_ref, o_ref, *, hd):
    acc = _dot(a_ref[...], w_ref[...])
    cos = c_ref[...]
    sin = s_ref[...]
    half = hd // 2
    for hh in range(acc.shape[1] // hd):
        x0 = acc[:, hh * hd:hh * hd + half]
        x1 = acc[:, hh * hd + half:(hh + 1) * hd]
        rot = flag_ref[:, hh * hd:hh * hd + half] > 0.5
        o_ref[:, hh * hd:hh * hd + half] = jnp.where(rot, x0 * cos - x1 * sin, x0).astype(o_ref.dtype)
        o_ref[:, hh * hd + half:(hh + 1) * hd] = jnp.where(rot, x0 * sin + x1 * cos, x1).astype(o_ref.dtype)


def _ret_proj(hn, w, flag, cos, sin, hd):
    m, d = hn.shape
    n = w.shape[1]
    tm, tn = _mm_tiles(m, n, 0)
    tn = max(tn, hd)
    half = hd // 2
    return pl.pallas_call(
        functools.partial(_ret_proj_kernel, hd=hd),
        grid=(m // tm, n // tn),
        in_specs=[
            pl.BlockSpec((tm, d), lambda i, j: (i, 0)),
            pl.BlockSpec((d, tn), lambda i, j: (0, j)),
            pl.BlockSpec((1, tn), lambda i, j: (0, j)),
            pl.BlockSpec((tm, half), lambda i, j: (i, 0)),
            pl.BlockSpec((tm, half), lambda i, j: (i, 0)),
        ],
        out_specs=pl.BlockSpec((tm, tn), lambda i, j: (i, j)),
        out_shape=jax.ShapeDtypeStruct((m, n), BF16),
        compiler_params=_params(("parallel", "parallel")),
        name="ret_proj",
    )(hn, w, flag.reshape(1, n), cos, sin)


def _ret_kernel(q_ref, k_ref, v_ref, lg_ref, o_ref, s_ref, dec_ref, xi_ref, zeta_ref, *, hp):
    d = pl.program_id(0)
    ch = q_ref.shape[0]
    hd = q_ref.shape[1] // hp

    @pl.when(pl.program_id(2) == 0)
    def _():
        s_ref[...] = jnp.zeros(s_ref.shape, F32)
        r = lax.broadcasted_iota(jnp.int32, (ch, ch), 0)
        c = lax.broadcasted_iota(jnp.int32, (ch, ch), 1)
        sgn = 1 - 2 * d
        rel = (r - c) * sgn
        seen = rel >= 0
        t = lax.broadcasted_iota(jnp.int32, (ch, hd), 0)
        pos = jnp.where(d == 0, t, ch - 1 - t).astype(F32)
        for hl in range(hp):
            lg = _log_sigmoid(lg_ref[hl])[:, 0:1]
            dec_ref[hl] = jnp.where(seen, jnp.exp(lg * jnp.where(seen, rel, 0).astype(F32)), 0.0)
            xi_ref[hl] = jnp.exp(lg * (pos + 1.0))
            zeta_ref[hl] = jnp.exp(lg * (ch - 1.0 - pos))

    for hl in range(hp):
        q = q_ref[:, hl * hd:(hl + 1) * hd]
        k = k_ref[:, hl * hd:(hl + 1) * hd]
        v = v_ref[:, hl * hd:(hl + 1) * hd]
        s_prev = s_ref[hl]
        gamma_chunk = jnp.exp(_log_sigmoid(lg_ref[hl])[:, 0:1] * ch)
        qk = lax.dot_general(q, k, (((1,), (1,)), ((), ())), preferred_element_type=F32)
        sc = (qk * dec_ref[hl]).astype(BF16)
        o_ref[:, hl * hd:(hl + 1) * hd] = _dot(sc, v) + _dot(q, s_prev.astype(BF16)) * xi_ref[hl]
        kz = (k.astype(F32) * zeta_ref[hl]).astype(BF16)
        s_ref[hl] = gamma_chunk * s_prev + lax.dot_general(kz, v, (((0,), (0,)), ((), ())),
                                                           preferred_element_type=F32)


def _ret_scan(qkv, decay_logit, n_heads, hd):
    m = qkv.shape[0]
    ch = SCAN_CHUNK
    n_blk = m // ch
    blk = _scan_block(n_blk)
    hp = SCAN_HEADS_PER_STEP
    hb = n_heads // hp
    lg = jnp.broadcast_to(decay_logit.astype(F32).reshape(2 * n_heads, 1, 1), (2 * n_heads, 1, LANES))
    return pl.pallas_call(
        functools.partial(_ret_kernel, hp=hp),
        grid=(2, hb, n_blk),
        in_specs=[
            pl.BlockSpec((ch, hp * hd), lambda d, h, c: (blk(d, c), h)),
            pl.BlockSpec((ch, hp * hd), lambda d, h, c: (blk(d, c), hb + h)),
            pl.BlockSpec((ch, hp * hd), lambda d, h, c: (blk(d, c), 2 * hb + h)),
            pl.BlockSpec((hp, 1, LANES), lambda d, h, c: (d * hb + h, 0, 0)),
        ],
        out_specs=pl.BlockSpec((None, ch, hp * hd), lambda d, h, c: (d, blk(d, c), h)),
        out_shape=jax.ShapeDtypeStruct((2, m, n_heads * hd), F32),
        scratch_shapes=[
            pltpu.VMEM((hp, hd, hd), F32),
            pltpu.VMEM((hp, ch, ch), F32),
            pltpu.VMEM((hp, ch, hd), F32),
            pltpu.VMEM((hp, ch, hd), F32),
        ],
        compiler_params=_params(("parallel", "parallel", "arbitrary")),
        name="ret_scan",
    )(qkv, qkv, qkv, lg)


def _split_pairs(w, n_heads, hd):
    lead = w.shape[:-1]
    return w.reshape(*lead, n_heads, hd // 2, 2).swapaxes(-1, -2).reshape(*lead, n_heads * hd)


def _rope_tables(n_lat, n_ctx, head_dim):
    t = jnp.arange(n_lat)
    row = (t // GRID_W).astype(F32)
    col = (t % GRID_W).astype(F32)
    axis_dim = head_dim // 2
    inv_freq = 1.0 / (ROPE_THETA ** (jnp.arange(0, axis_dim, 2, dtype=F32) / axis_dim))
    ang = jnp.concatenate([row[:, None] * inv_freq, col[:, None] * inv_freq], axis=-1)
    cos = jnp.concatenate([jnp.ones((n_ctx, head_dim // 2), F32), jnp.cos(ang)], axis=0)
    sin = jnp.concatenate([jnp.zeros((n_ctx, head_dim // 2), F32), jnp.sin(ang)], axis=0)
    return cos, sin


def kernel(x, c, ctx, c_ctx, ada_w, ada_b, norm1_g, norm2_g, ffn_w_up, ffn_conv_w, ffn_conv_b, ffn_w_down, pool_w, pool_scale, attn_wq, attn_wk, attn_wv, attn_wo, attn_q_norm, attn_k_norm, mlstm_wq, mlstm_wk, mlstm_wv, mlstm_w_gates, mlstm_b_gates, mlstm_w_ogate, mlstm_out_norm, mlstm_wo, ret_wq, ret_wk, ret_wv, ret_wg, ret_decay_logit, ret_out_norm, ret_wo):
    batch, n_lat, d = x.shape
    n_ctx = ctx.shape[1]
    depth = ada_w.shape[0]
    assert batch == 1 and n_ctx % BF16_SUBLANES == 0 and n_lat % n_ctx == 0

    z = jnp.concatenate([ctx[0], x[0]], axis=0)
    cc = jnp.zeros((8, d), F32).at[0].set(c[0]).at[1].set(c_ctx)
    mods = _ada_mods(cc, ada_w, ada_b)

    for i in range(depth):
        kind, j = i % 4, i // 4
        mod = mods[i, :2]
        if kind == 0:
            pooled = _pool_pre(z, norm1_g[i], mod, n_ctx)
            z = _mm_resid(pooled, pool_w[j].astype(BF16), z, mod, 2, n_ctx, colscale=pool_scale[j],
                          groups=len(POOL_WINDOWS), name="pool_mix")
        elif kind == 1:
            hd = ATTN_HEAD_DIM
            qh = attn_wq.shape[2] // hd
            kvh = attn_wk.shape[2] // hd
            hn = _norm_mod(z, norm1_g[i], mod, 0, n_ctx)
            w = jnp.concatenate([_split_pairs(attn_wq[j], qh, hd), _split_pairs(attn_wk[j], kvh, hd),
                                 attn_wv[j]], axis=1).astype(BF16)
            gain = jnp.concatenate([jnp.tile(_split_pairs(attn_q_norm[j], 1, hd) * (hd ** -0.5 * LOG2_E), qh),
                                    jnp.tile(_split_pairs(attn_k_norm[j], 1, hd), kvh),
                                    jnp.ones((kvh * hd,), F32)])
            flag = jnp.concatenate([jnp.ones(((qh + kvh) * hd,), F32), jnp.zeros((kvh * hd,), F32)])
            cos, sin = _rope_tables(n_lat, n_ctx, hd)
            cos2 = jnp.concatenate([cos, cos], axis=1)
            sin2 = jnp.concatenate([-sin, sin], axis=1)
            qkv = _attn_proj(hn, w, gain, flag, cos2, sin2)
            o = _flash(qkv, n_ctx, qh)
            z = _mm_resid(o, attn_wo[j].astype(BF16), z, mod, 2, n_ctx, name="attn_out")
        elif kind == 2:
            nh = MLSTM_HEADS
            dk = mlstm_wq.shape[2] // nh
            dv = mlstm_wv.shape[2] // nh
            hn = _norm_mod(z, norm1_g[i], mod, 0, n_ctx)
            w = jnp.concatenate([mlstm_wq[j], mlstm_wk[j] * dk ** -0.5, mlstm_wv[j]], axis=1).astype(BF16)
            qkv = _mm_plain(hn, w, name="mlstm_proj")
            wg = jnp.zeros((d, LANES), F32).at[:, :4 * nh].set(mlstm_w_gates[j]).astype(BF16)
            bg = jnp.zeros((LANES,), F32).at[:4 * nh].set(mlstm_b_gates[j])
            g2 = _gate_prep(_mm_bias(hn, wg, bg, name="mlstm_gates"), nh)
            g2t = g2.T.reshape(LANES, 1, -1)
            hfb = _mlstm_scan(qkv, g2, g2t, nh, dk, dv)
            gated = _mm_gated_norm(hn, mlstm_w_ogate[j].astype(BF16), hfb, mlstm_out_norm[j], dv,
                                   _sigmoid, "mlstm_ogate")
            z = _mm_resid(gated, mlstm_wo[j].astype(BF16), z, mod, 2, n_ctx, name="mlstm_out")
        else:
            nh = RET_HEADS
            hd = ret_wq.shape[2] // nh
            hn = _norm_mod(z, norm1_g[i], mod, 0, n_ctx)
            w = jnp.concatenate([_split_pairs(ret_wq[j], nh, hd), _split_pairs(ret_wk[j] * hd ** -0.5, nh, hd),
                                 ret_wv[j]], axis=1).astype(BF16)
            flag = jnp.concatenate([jnp.ones((2 * nh * hd,), F32), jnp.zeros((nh * hd,), F32)])
            cos, sin = _rope_tables(n_lat, n_ctx, hd)
            qkv = _ret_proj(hn, w, flag, cos, sin, hd)
            ofb = _ret_scan(qkv, ret_decay_logit[j], nh, hd)
            gated = _mm_gated_norm(hn, ret_wg[j].astype(BF16), ofb, ret_out_norm[j], hd, _silu, "ret_gate")
            z = _mm_resid(gated, ret_wo[j].astype(BF16), z, mod, 2, n_ctx, name="ret_out")
        z = _conv_ffn(z, norm2_g[i], mod, ffn_w_up[i].astype(BF16), ffn_conv_w[i], ffn_conv_b[i],
                      ffn_w_down[i].astype(BF16), n_ctx)
    return z[n_ctx:][None]
```

```python
import functools

import jax
import jax.numpy as jnp
from jax import lax
from jax.experimental import pallas as pl
from jax.experimental.pallas import tpu as pltpu

F32 = jnp.float32
BF16 = jnp.bfloat16

EPS = 1e-6
GRID_W = 64
ROPE_THETA = 10000.0
N_MOD = 6
POOL_WINDOWS = (2, 4, 8, 16)
ATTN_HEAD_DIM = 128
ATTN_GROUP = 4
MLSTM_HEADS = 8
RET_HEADS = 16
CONV_WIDTH = 3
LOG2_E = 1.4426950408889634

VMEM_LIMIT_BYTES = 56 * 1024 * 1024
BF16_SUBLANES = 16
F32_SUBLANES = 8
LANES = 128
SCAN_CHUNK = 256
SCAN_HEADS_PER_STEP = 2
NORM_UNROLL = 4
POOL_HALO = 8
POOL_ROWS = 32


def _pick(n, candidates):
    for c in candidates:
        if c <= n and n % c == 0:
            return c
    return n


def _params(sem):
    return pltpu.CompilerParams(dimension_semantics=sem, vmem_limit_bytes=VMEM_LIMIT_BYTES)


def _silu(x):
    return x * (1.0 / (1.0 + jnp.exp(-x)))


def _sigmoid(x):
    return 1.0 / (1.0 + jnp.exp(-x))


def _log_sigmoid(x):
    return jnp.minimum(x, 0.0) - jnp.log(1.0 + jnp.exp(-jnp.abs(x)))


def _row_ids(i, tm):
    return i * tm + lax.broadcasted_iota(jnp.int32, (tm, 1), 0)


def _sel_rows(is_ctx, ref2):
    return jnp.where(is_ctx, ref2[1:2, :], ref2[0:1, :])


def _dot(a, b):
    return jnp.dot(a, b, preferred_element_type=F32)


def _ada_kernel(a_ref, w_ref, b_ref, o_ref):
    a = _silu(a_ref[...]).astype(BF16)
    o_ref[...] = _dot(a, w_ref[...].astype(BF16)) + b_ref[...]


def _ada_mods(cc, ada_w, ada_b):
    depth, d, n = ada_w.shape
    tn = _pick(n, (512, 256, 128))
    rows = cc.shape[0]
    return pl.pallas_call(
        _ada_kernel,
        grid=(depth, n // tn),
        in_specs=[
            pl.BlockSpec((rows, d), lambda l, j: (0, 0)),
            pl.BlockSpec((None, d, tn), lambda l, j: (l, 0, j)),
            pl.BlockSpec((None, 1, tn), lambda l, j: (l, 0, j)),
        ],
        out_specs=pl.BlockSpec((None, rows, tn), lambda l, j: (l, 0, j)),
        out_shape=jax.ShapeDtypeStruct((depth, rows, n), F32),
        compiler_params=_params(("parallel", "parallel")),
        name="ada_mods",
    )(cc, ada_w, ada_b.reshape(depth, 1, n))


def _norm_kernel(x_ref, g_ref, sh_ref, sc_ref, o_ref, a_ref):
    a_ref[...] = g_ref[...] * (1.0 + sc_ref[...])
    s8 = F32_SUBLANES

    def body(r, carry):
        base = pl.multiple_of(r * BF16_SUBLANES, BF16_SUBLANES)
        ys = []
        for part in range(BF16_SUBLANES // s8):
            x = x_ref[pl.ds(pl.multiple_of(base + part * s8, s8), s8), :]
            ms = jnp.mean(x * x, axis=-1, keepdims=True)
            ys.append(x * lax.rsqrt(ms + EPS) * a_ref[...] + sh_ref[...])
        o_ref[pl.ds(base, BF16_SUBLANES), :] = jnp.concatenate(ys, axis=0).astype(o_ref.dtype)
        return carry

    lax.fori_loop(0, x_ref.shape[0] // BF16_SUBLANES, body, 0, unroll=NORM_UNROLL)


def _norm_mod(z, g, mod, k_shift, n_ctx):
    m, d = z.shape
    s8 = F32_SUBLANES
    tm = _pick(n_ctx, (256, 128, 64, 32, 16))
    mod8 = jnp.repeat(mod, s8, axis=0)
    vec_blk = lambda i: jnp.where(i * tm < n_ctx, 1, 0)
    return pl.pallas_call(
        _norm_kernel,
        grid=(m // tm,),
        in_specs=[
            pl.BlockSpec((tm, d), lambda i: (i, 0)),
            pl.BlockSpec((s8, d), lambda i: (0, 0)),
            pl.BlockSpec((s8, d), lambda i: (vec_blk(i), k_shift)),
            pl.BlockSpec((s8, d), lambda i: (vec_blk(i), k_shift + 1)),
        ],
        out_specs=pl.BlockSpec((tm, d), lambda i: (i, 0)),
        out_shape=jax.ShapeDtypeStruct((m, d), BF16),
        scratch_shapes=[pltpu.VMEM((s8, d), F32)],
        compiler_params=_params(("parallel",)),
        name="norm_mod",
    )(z, jnp.broadcast_to(g.reshape(1, d), (s8, d)), mod8, mod8)


def _mm_tiles(m, n, n_ctx):
    tm = _pick(m, (768, 512, 256, 128))
    tn = _pick(n, (512, 256, 128))
    return tm, tn


def _resid_kernel(a_ref, w_ref, x_ref, g_ref, cs_ref, o_ref, *, tm, n_ctx):
    is_ctx = _row_ids(pl.program_id(0), tm) < n_ctx
    acc = _dot(a_ref[...], w_ref[...]) * cs_ref[...]
    o_ref[...] = x_ref[...] + _sel_rows(is_ctx, g_ref) * acc


def _mm_resid(a, w, x, mod, k_gate, n_ctx, colscale=None, groups=1, layer=None, name="mm_resid"):
    m, ka = a.shape
    n = x.shape[1]
    kg = ka // groups
    ng = n // groups
    tm, tn = _mm_tiles(m, ng, n_ctx)
    jn = ng // tn
    if colscale is None:
        colscale = jnp.ones((n,), F32)
    if layer is not None:
        w_spec = pl.BlockSpec((None, kg, tn), lambda i, j: (layer, 0, j))
    elif groups == 1:
        w_spec = pl.BlockSpec((kg, tn), lambda i, j: (0, j))
    else:
        w_spec = pl.BlockSpec((None, kg, tn), lambda i, j: (j // jn, 0, j % jn))
    return pl.pallas_call(
        functools.partial(_resid_kernel, tm=tm, n_ctx=n_ctx),
        grid=(m // tm, n // tn),
        in_specs=[
            pl.BlockSpec((tm, kg), lambda i, j: (i, j // jn)),
            w_spec,
            pl.BlockSpec((tm, tn), lambda i, j: (i, j)),
            pl.BlockSpec((2, tn), lambda i, j: (0, k_gate * (n // tn) + j)),
            pl.BlockSpec((1, tn), lambda i, j: (0, j)),
        ],
        out_specs=pl.BlockSpec((tm, tn), lambda i, j: (i, j)),
        out_shape=jax.ShapeDtypeStruct((m, n), F32),
        compiler_params=_params(("parallel", "parallel")),
        name=name,
    )(a, w, x, mod, colscale.reshape(1, n))


def _bias_kernel(a_ref, w_ref, b_ref, o_ref):
    o_ref[...] = _dot(a_ref[...], w_ref[...]) + b_ref[...]


def _mm_bias(a, w, b, name="mm_bias"):
    m, k = a.shape
    n = w.shape[1]
    tm, tn = _mm_tiles(m, n, 0)
    return pl.pallas_call(
        _bias_kernel,
        grid=(m // tm, n // tn),
        in_specs=[
            pl.BlockSpec((tm, k), lambda i, j: (i, 0)),
            pl.BlockSpec((k, tn), lambda i, j: (0, j)),
            pl.BlockSpec((1, tn), lambda i, j: (0, j)),
        ],
        out_specs=pl.BlockSpec((tm, tn), lambda i, j: (i, j)),
        out_shape=jax.ShapeDtypeStruct((m, n), F32),
        compiler_params=_params(("parallel", "parallel")),
        name=name,
    )(a, w, b.reshape(1, n))


def _plain_kernel(a_ref, w_ref, o_ref):
    o_ref[...] = _dot(a_ref[...], w_ref[...]).astype(o_ref.dtype)


def _mm_plain(a, w, name="mm_plain"):
    m, k = a.shape
    n = w.shape[1]
    tm, tn = _mm_tiles(m, n, 0)
    return pl.pallas_call(
        _plain_kernel,
        grid=(m // tm, n // tn),
        in_specs=[
            pl.BlockSpec((tm, k), lambda i, j: (i, 0)),
            pl.BlockSpec((k, tn), lambda i, j: (0, j)),
        ],
        out_specs=pl.BlockSpec((tm, tn), lambda i, j: (i, j)),
        out_shape=jax.ShapeDtypeStruct((m, n), BF16),
        compiler_params=_params(("parallel", "parallel")),
        name=name,
    )(a, w)


def _ffn_up_kernel(a_ref, ap_ref, an_ref, wg_ref, wv_ref, cwg_ref, cwv_ref, cbg_ref, cbv_ref,
                   o_ref, ext_ref, *, tm, n_ctx, m_total):
    i = pl.program_id(0)
    h = BF16_SUBLANES

    @pl.when(pl.program_id(1) == 0)
    def _():
        ext_ref[0:h, :] = ap_ref[...]
        ext_ref[h:h + tm, :] = a_ref[...]
        ext_ref[h + tm:h + tm + h, :] = an_ref[...]

    rows = _row_ids(i, tm)
    seq_first = (rows == 0) | (rows == n_ctx)
    seq_last = (rows == n_ctx - 1) | (rows == m_total - 1)
    a_ext = ext_ref[...]
    n_ext = tm + 2 * h

    def conv(w_ref, cw_ref, cb_ref):
        u = _dot(a_ext, w_ref[...])
        u_prev = pltpu.roll(u, 1, axis=0)[h:h + tm]
        u_next = pltpu.roll(u, n_ext - 1, axis=0)[h:h + tm]
        u_prev = jnp.where(seq_first, 0.0, u_prev)
        u_next = jnp.where(seq_last, 0.0, u_next)
        cw = cw_ref[...]
        return u_prev * cw[0:1] + u[h:h + tm] * cw[1:2] + u_next * cw[2:3] + cb_ref[...]

    gate = conv(wg_ref, cwg_ref, cbg_ref)
    val = conv(wv_ref, cwv_ref, cbv_ref)
    o_ref[...] = (_silu(gate) * val).astype(o_ref.dtype)


def _ffn_up(hn, w_up, layer, conv_w, conv_b, n_ctx):
    m, d = hn.shape
    dff = w_up.shape[2] // 2
    tm, tn = _mm_tiles(m, dff, n_ctx)
    jn = dff // tn
    h = BF16_SUBLANES
    rb = tm // h
    last_rb = m // h - 1
    return pl.pallas_call(
        functools.partial(_ffn_up_kernel, tm=tm, n_ctx=n_ctx, m_total=m),
        grid=(m // tm, jn),
        in_specs=[
            pl.BlockSpec((tm, d), lambda i, j: (i, 0)),
            pl.BlockSpec((h, d), lambda i, j: (jnp.maximum(i * rb - 1, 0), 0)),
            pl.BlockSpec((h, d), lambda i, j: (jnp.minimum((i + 1) * rb, last_rb), 0)),
            pl.BlockSpec((None, d, tn), lambda i, j: (layer, 0, j)),
            pl.BlockSpec((None, d, tn), lambda i, j: (layer, 0, jn + j)),
            pl.BlockSpec((CONV_WIDTH, tn), lambda i, j: (0, j)),
            pl.BlockSpec((CONV_WIDTH, tn), lambda i, j: (0, jn + j)),
            pl.BlockSpec((1, tn), lambda i, j: (0, j)),
            pl.BlockSpec((1, tn), lambda i, j: (0, jn + j)),
        ],
        out_specs=pl.BlockSpec((tm, tn), lambda i, j: (i, j)),
        out_shape=jax.ShapeDtypeStruct((m, dff), BF16),
        scratch_shapes=[pltpu.VMEM((tm + 2 * h, d), BF16)],
        compiler_params=_params(("parallel", "arbitrary")),
        name="ffn_up",
    )(hn, hn, hn, w_up, w_up, conv_w, conv_w, conv_b.reshape(1, -1), conv_b.reshape(1, -1))


def _conv_ffn(z, norm_g, mod, w_up, w_down, layer, conv_w, conv_b, n_ctx):
    hn = _norm_mod(z, norm_g, mod, 3, n_ctx)
    act = _ffn_up(hn, w_up, layer, conv_w, conv_b, n_ctx)
    return _mm_resid(act, w_down, z, mod, 5, n_ctx, layer=layer, name="ffn_down")


def _pool_kernel(x_ref, xp_ref, xn_ref, g_ref, sh_ref, sc_ref, o_ref, s_ref, a_ref, *, tm, n_ctx, m_total):
    i = pl.program_id(0)
    p = POOL_HALO
    d = x_ref.shape[1]
    cg = d // len(POOL_WINDOWS)
    a_ref[...] = g_ref[...] * (1.0 + sc_ref[...])

    def modulated(x):
        ms = jnp.mean(x * x, axis=-1, keepdims=True)
        return x * lax.rsqrt(ms + EPS) * a_ref[...] + sh_ref[...]

    start = i * tm
    stop = start + tm
    prev_ok = jnp.logical_and(start != 0, start != n_ctx)
    next_ok = jnp.logical_and(stop != n_ctx, stop != m_total)
    s_ref[0:p, :] = jnp.where(prev_ok, modulated(xp_ref[...]), 0.0)
    s_ref[p + tm:p + tm + p, :] = jnp.where(next_ok, modulated(xn_ref[...]), 0.0)

    def body(r, carry):
        base = pl.multiple_of(r * p, p)
        s_ref[pl.ds(p + base, p), :] = modulated(x_ref[pl.ds(base, p), :])
        return carry

    lax.fori_loop(0, tm // p, body, 0, unroll=NORM_UNROLL)

    in_ctx = start < n_ctx
    t0 = jnp.where(in_ctx, start, start - n_ctx)
    t_len = jnp.where(in_ctx, n_ctx, m_total - n_ctx)
    nr = POOL_ROWS
    for rb in range(0, tm, nr):
        t = t0 + rb + lax.broadcasted_iota(jnp.int32, (nr, 1), 0)
        for gi, w in enumerate(POOL_WINDOWS):
            cols = slice(gi * cg, (gi + 1) * cg)
            acc = s_ref[p + rb - w // 2:p + rb - w // 2 + nr, cols]
            for k in range(1 - w // 2, w - w // 2):
                acc = acc + s_ref[p + rb + k:p + rb + k + nr, cols]
            lo = jnp.maximum(t - w // 2, 0)
            hi = jnp.minimum(t + w - w // 2, t_len)
            cnt = (hi - lo).astype(F32)
            o_ref[rb:rb + nr, cols] = (acc / cnt - s_ref[p + rb:p + rb + nr, cols]).astype(o_ref.dtype)


def _pool_pre(z, g, mod, n_ctx):
    m, d = z.shape
    tm = _pick(n_ctx, (256, 128, 64, 32))
    p = POOL_HALO
    rb = tm // p
    last_rb = m // p - 1
    mod8 = jnp.repeat(mod, p, axis=0)
    vec_blk = lambda i: jnp.where(i * tm < n_ctx, 1, 0)
    return pl.pallas_call(
        functools.partial(_pool_kernel, tm=tm, n_ctx=n_ctx, m_total=m),
        grid=(m // tm,),
        in_specs=[
            pl.BlockSpec((tm, d), lambda i: (i, 0)),
            pl.BlockSpec((p, d), lambda i: (jnp.maximum(i * rb - 1, 0), 0)),
            pl.BlockSpec((p, d), lambda i: (jnp.minimum((i + 1) * rb, last_rb), 0)),
            pl.BlockSpec((p, d), lambda i: (0, 0)),
            pl.BlockSpec((p, d), lambda i: (vec_blk(i), 0)),
            pl.BlockSpec((p, d), lambda i: (vec_blk(i), 1)),
        ],
        out_specs=pl.BlockSpec((tm, d), lambda i: (i, 0)),
        out_shape=jax.ShapeDtypeStruct((m, d), BF16),
        scratch_shapes=[pltpu.VMEM((tm + 2 * p, d), F32), pltpu.VMEM((p, d), F32)],
        compiler_params=_params(("parallel",)),
        name="pool_pre",
    )(z, z, z, jnp.broadcast_to(g.reshape(1, d), (p, d)), mod8, mod8)


def _pair_cast_kernel(w_ref, o_ref, *, hd, scale):
    r = lax.broadcasted_iota(jnp.int32, (hd, hd), 0)
    c = lax.broadcasted_iota(jnp.int32, (hd, hd), 1)
    src_col = jnp.where(c < hd // 2, 2 * c, 2 * (c - hd // 2) + 1)
    perm = jnp.where(r == src_col, 1.0, 0.0).astype(BF16)
    for hh in range(w_ref.shape[1] // hd):
        cols = slice(hh * hd, (hh + 1) * hd)
        wh = (w_ref[:, cols] * scale).astype(BF16)
        o_ref[:, cols] = _dot(wh, perm).astype(BF16)


def _pair_cast(w, hd, scale=1.0):
    k, n = w.shape
    tk = _pick(k, (1024, 512, 256, 128))
    tn = _pick(n, (1024, 512, 256, 128))
    return pl.pallas_call(
        functools.partial(_pair_cast_kernel, hd=hd, scale=scale),
        grid=(k // tk, n // tn),
        in_specs=[pl.BlockSpec((tk, tn), lambda i, j: (i, j))],
        out_specs=pl.BlockSpec((tk, tn), lambda i, j: (i, j)),
        out_shape=jax.ShapeDtypeStruct((k, n), BF16),
        compiler_params=_params(("parallel", "parallel")),
        name="pair_cast",
    )(w)


def _head_proj_kernel(a_ref, w_ref, g_ref, c_ref, s_ref, o_ref, *, hd, normalize):
    acc = _dot(a_ref[...], w_ref[...])
    cos = c_ref[...]
    sin = s_ref[...]
    mean_w = jnp.full((2 * hd, hd), 1.0 / hd, BF16)
    for hh in range(acc.shape[1] // hd):
        cols = slice(hh * hd, (hh + 1) * hd)
        y = acc[:, cols]
        if normalize:
            sq = y * y
            hi = sq.astype(BF16)
            lo = (sq - hi.astype(F32)).astype(BF16)
            ms = _dot(jnp.concatenate([hi, lo], axis=1), mean_w)
            y = y * lax.rsqrt(ms + EPS) * g_ref[:, cols]
        o_ref[:, cols] = (y * cos + pltpu.roll(y, hd // 2, axis=1) * sin).astype(o_ref.dtype)


def _head_proj(hn, w, gain, cos, sin, hd, normalize, name):
    m, d = hn.shape
    n = w.shape[1]
    tm, tn = _mm_tiles(m, n, 0)
    tn = max(tn, hd)
    return pl.pallas_call(
        functools.partial(_head_proj_kernel, hd=hd, normalize=normalize),
        grid=(m // tm, n // tn),
        in_specs=[
            pl.BlockSpec((tm, d), lambda i, j: (i, 0)),
            pl.BlockSpec((d, tn), lambda i, j: (0, j)),
            pl.BlockSpec((1, tn), lambda i, j: (0, j)),
            pl.BlockSpec((tm, hd), lambda i, j: (i, 0)),
            pl.BlockSpec((tm, hd), lambda i, j: (i, 0)),
        ],
        out_specs=pl.BlockSpec((tm, tn), lambda i, j: (i, j)),
        out_shape=jax.ShapeDtypeStruct((m, n), BF16),
        compiler_params=_params(("parallel", "parallel")),
        name=name,
    )(hn, w, gain.reshape(1, n), cos, sin)


def _flash_kernel(q_ref, kt_ref, v_ref, o_ref, qs_ref, vx_ref, *, tq, chunk):
    hd = ATTN_HEAD_DIM
    n_keys = v_ref.shape[0]

    @pl.when(pl.program_id(1) == 0)
    def _():
        vx_ref[:, 0:hd] = v_ref[...]
        lane = lax.broadcasted_iota(jnp.int32, (n_keys, hd), 1)
        vx_ref[:, hd:2 * hd] = jnp.where(lane == 0, 1.0, 0.0).astype(BF16)

    for gi in range(ATTN_GROUP):
        qs_ref[gi * tq:(gi + 1) * tq, :] = q_ref[:, gi * hd:(gi + 1) * hd]
    q = qs_ref[...]
    m_run = None
    acc = None
    for start in range(0, n_keys, chunk):
        s = _dot(q, kt_ref[:, start:start + chunk])
        m_c = jnp.max(s, axis=-1, keepdims=True)
        m_new = m_c if m_run is None else jnp.maximum(m_run, m_c)
        p = jnp.exp2(s - m_new).astype(BF16)
        pv = _dot(p, vx_ref[start:start + chunk, :])
        acc = pv if acc is None else jnp.exp2(m_run - m_new) * acc + pv
        m_run = m_new
    out = acc[:, 0:hd] / acc[:, hd:hd + 1]
    for gi in range(ATTN_GROUP):
        o_ref[:, gi * hd:(gi + 1) * hd] = out[gi * tq:(gi + 1) * tq].astype(o_ref.dtype)


def _flash_part(q, kt, v, row0, n_rows, n_keys):
    hd = ATTN_HEAD_DIM
    kvh = v.shape[1] // hd
    tq = _pick(n_rows, (256, 128, 64, 32, 16))
    chunk = _pick(n_keys, (1408, 1024, 768, 512, 256, 128))
    gw = ATTN_GROUP * hd
    r0 = row0 // tq
    return pl.pallas_call(
        functools.partial(_flash_kernel, tq=tq, chunk=chunk),
        grid=(kvh, n_rows // tq),
        in_specs=[
            pl.BlockSpec((tq, gw), lambda g, i: (r0 + i, g)),
            pl.BlockSpec((hd, n_keys), lambda g, i: (g, 0)),
            pl.BlockSpec((n_keys, hd), lambda g, i: (0, g)),
        ],
        out_specs=pl.BlockSpec((tq, gw), lambda g, i: (i, g)),
        out_shape=jax.ShapeDtypeStruct((n_rows, q.shape[1]), BF16),
        scratch_shapes=[
            pltpu.VMEM((ATTN_GROUP * tq, hd), BF16),
            pltpu.VMEM((n_keys, 2 * hd), BF16),
        ],
        compiler_params=_params(("parallel", "arbitrary")),
        name="flash_attn",
    )(q, kt, v)


def _flash(q, k, v, n_ctx):
    m = q.shape[0]
    kt = k.T
    o_lat = _flash_part(q, kt, v, n_ctx, m - n_ctx, m)
    o_ctx = _flash_part(q, kt, v, 0, n_ctx, n_ctx)
    return jnp.concatenate([o_ctx, o_lat], axis=0)


def _gate_prep_kernel(x_ref, o_ref, *, n_heads):
    x = x_ref[...]
    rows = x.shape[0]
    col = lax.broadcasted_iota(jnp.int32, x.shape, 1)
    is_f = jnp.logical_and((col // n_heads) % 2 == 1, col < 4 * n_heads)
    backward = col >= 2 * n_heads
    logf = _log_sigmoid(x)
    r = lax.broadcasted_iota(jnp.int32, (rows, rows), 0)
    c = lax.broadcasted_iota(jnp.int32, (rows, rows), 1)
    tril = (c <= r).astype(F32)
    triu = (c >= r).astype(F32)
    fwd = jnp.dot(tril, logf, preferred_element_type=F32, precision=lax.Precision.HIGHEST)
    bwd = jnp.dot(triu, logf, preferred_element_type=F32, precision=lax.Precision.HIGHEST)
    o_ref[...] = jnp.where(is_f, jnp.where(backward, bwd, fwd), x)


def _gate_prep(gates, n_heads):
    m, n = gates.shape
    ch = SCAN_CHUNK
    return pl.pallas_call(
        functools.partial(_gate_prep_kernel, n_heads=n_heads),
        grid=(m // ch,),
        in_specs=[pl.BlockSpec((ch, n), lambda c: (c, 0))],
        out_specs=pl.BlockSpec((ch, n), lambda c: (c, 0)),
        out_shape=jax.ShapeDtypeStruct((m, n), F32),
        compiler_params=_params(("parallel",)),
        name="gate_prep",
    )(gates)


def _scan_block(n_blk):
    def blk(d, c):
        return jnp.where(c == 0, 0, jnp.where(d == 0, c, n_blk - c))
    return blk


def _mlstm_kernel(q_ref, k_ref, v_ref, gc_ref, ir_ref, br_ref, o_ref, s_ref, n_ref, m_ref, *, n_heads, hp):
    d = pl.program_id(0)
    ch = q_ref.shape[0]
    dk = q_ref.shape[1] // hp
    dv = v_ref.shape[1] // hp

    @pl.when(pl.program_id(2) == 0)
    def _():
        s_ref[...] = jnp.zeros(s_ref.shape, F32)
        n_ref[...] = jnp.zeros(n_ref.shape, F32)
        m_ref[...] = jnp.zeros(m_ref.shape, F32)

    gc = gc_ref[...]
    lane = lax.broadcasted_iota(jnp.int32, gc.shape, 1)
    r = lax.broadcasted_iota(jnp.int32, (ch, ch), 0)
    c = lax.broadcasted_iota(jnp.int32, (ch, ch), 1)
    sgn = 1 - 2 * d
    seen = (r - c) * sgn >= 0

    for hl in range(hp):
        q = q_ref[:, hl * dk:(hl + 1) * dk]
        k = k_ref[:, hl * dk:(hl + 1) * dk]
        v = v_ref[:, hl * dv:(hl + 1) * dv]
        base = d * 2 * n_heads + pl.program_id(1) * hp + hl
        i_col = jnp.sum(jnp.where(lane == base, gc, 0.0), axis=1, keepdims=True)
        b_col = jnp.sum(jnp.where(lane == base + n_heads, gc, 0.0), axis=1, keepdims=True)
        i_row = ir_ref[hl]
        b_row = br_ref[hl]
        m_prev = m_ref[hl]
        s_prev = s_ref[hl]
        n_prev = n_ref[hl]

        log_d = jnp.where(seen, b_col - b_row + i_row, -jnp.inf)
        log_prev = b_col + m_prev
        m_t = jnp.maximum(log_prev, jnp.max(log_d, axis=-1, keepdims=True))
        qk = lax.dot_general(q, k, (((1,), (1,)), ((), ())), preferred_element_type=F32)
        s = qk * jnp.exp(log_d - m_t)
        w_prev = jnp.exp(log_prev - m_t)
        num = _dot(s.astype(BF16), v) + w_prev * _dot(q, s_prev.astype(BF16))
        den = (jnp.sum(s, axis=-1, keepdims=True)
               + w_prev * jnp.sum(q.astype(F32) * n_prev, axis=-1, keepdims=True))
        o_ref[:, hl * dv:(hl + 1) * dv] = num / jnp.maximum(jnp.abs(den), jnp.exp(-m_t))

        b_last = jnp.where(d == 0, b_row[:, ch - 1:ch], b_row[:, 0:1])
        log_w_row = b_last - b_row + i_row
        m_new = jnp.maximum(b_last + m_prev, jnp.max(log_w_row, axis=-1, keepdims=True))
        w_col = jnp.exp(b_last - b_col + i_col - m_new)
        decay = jnp.exp(b_last + m_prev - m_new)
        wv = (w_col * v.astype(F32)).astype(BF16)
        s_ref[hl] = decay * s_prev + lax.dot_general(k, wv, (((0,), (0,)), ((), ())),
                                                     preferred_element_type=F32)
        n_ref[hl] = decay * n_prev + jnp.sum(w_col * k.astype(F32), axis=0, keepdims=True)
        m_ref[hl] = m_new


def _mlstm_scan(q, k, v, g2, g2t, n_heads, dk, dv):
    m = q.shape[0]
    ch = SCAN_CHUNK
    n_blk = m // ch
    blk = _scan_block(n_blk)
    hp = SCAN_HEADS_PER_STEP
    hb = n_heads // hp
    return pl.pallas_call(
        functools.partial(_mlstm_kernel, n_heads=n_heads, hp=hp),
        grid=(2, hb, n_blk),
        in_specs=[
            pl.BlockSpec((ch, hp * dk), lambda d, h, c: (blk(d, c), h)),
            pl.BlockSpec((ch, hp * dk), lambda d, h, c: (blk(d, c), h)),
            pl.BlockSpec((ch, hp * dv), lambda d, h, c: (blk(d, c), h)),
            pl.BlockSpec((ch, g2.shape[1]), lambda d, h, c: (blk(d, c), 0)),
            pl.BlockSpec((hp, 1, ch), lambda d, h, c: (d * 2 * hb + h, 0, blk(d, c))),
            pl.BlockSpec((hp, 1, ch), lambda d, h, c: (d * 2 * hb + hb + h, 0, blk(d, c))),
        ],
        out_specs=pl.BlockSpec((None, ch, hp * dv), lambda d, h, c: (d, blk(d, c), h)),
        out_shape=jax.ShapeDtypeStruct((2, m, n_heads * dv), F32),
        scratch_shapes=[
            pltpu.VMEM((hp, dk, dv), F32),
            pltpu.VMEM((hp, 1, dk), F32),
            pltpu.VMEM((hp, 1, 1), F32),
        ],
        compiler_params=_params(("parallel", "parallel", "arbitrary")),
        name="mlstm_scan",
    )(q, k, v, g2, g2t, g2t)


def _gated_norm_kernel(a_ref, w_ref, hf_ref, hb_ref, g_ref, o_ref, *, hw, act):
    acc = _dot(a_ref[...], w_ref[...])
    for hh in range(acc.shape[1] // hw):
        cols = slice(hh * hw, (hh + 1) * hw)
        y = hf_ref[:, cols] + hb_ref[:, cols]
        ms = jnp.mean(y * y, axis=-1, keepdims=True)
        y = y * lax.rsqrt(ms + EPS) * g_ref[:, cols]
        o_ref[:, cols] = (act(acc[:, cols]) * y).astype(o_ref.dtype)


def _mm_gated_norm(hn, w, hfb, gain, hw, act, name):
    m, d = hn.shape
    n = w.shape[1]
    tm, tn = _mm_tiles(m, n, 0)
    tn = max(tn, hw)
    return pl.pallas_call(
        functools.partial(_gated_norm_kernel, hw=hw, act=act),
        grid=(m // tm, n // tn),
        in_specs=[
            pl.BlockSpec((tm, d), lambda i, j: (i, 0)),
            pl.BlockSpec((d, tn), lambda i, j: (0, j)),
            pl.BlockSpec((None, tm, tn), lambda i, j: (0, i, j)),
            pl.BlockSpec((None, tm, tn), lambda i, j: (1, i, j)),
            pl.BlockSpec((1, tn), lambda i, j: (0, j)),
        ],
        out_specs=pl.BlockSpec((tm, tn), lambda i, j: (i, j)),
        out_shape=jax.ShapeDtypeStruct((m, n), BF16),
        compiler_params=_params(("parallel", "parallel")),
        name=name,
    )(hn, w, hfb, hfb, gain.reshape(1, n))


def _ret_kernel(q_ref, k_ref, v_ref, lg_ref, o_ref, s_ref, dec_ref, xi_ref, zeta_ref, *, hp):
    d = pl.program_id(0)
    ch = q_ref.shape[0]
    hd = q_ref.shape[1] // hp

    @pl.when(pl.program_id(2) == 0)
    def _():
        s_ref[...] = jnp.zeros(s_ref.shape, F32)
        r = lax.broadcasted_iota(jnp.int32, (ch, ch), 0)
        c = lax.broadcasted_iota(jnp.int32, (ch, ch), 1)
        sgn = 1 - 2 * d
        rel = (r - c) * sgn
        seen = rel >= 0
        t = lax.broadcasted_iota(jnp.int32, (ch, hd), 0)
        pos = jnp.where(d == 0, t, ch - 1 - t).astype(F32)
        for hl in range(hp):
            lg = _log_sigmoid(lg_ref[hl])[:, 0:1]
            dec_ref[hl] = jnp.where(seen, jnp.exp(lg * jnp.where(seen, rel, 0).astype(F32)), 0.0)
            xi_ref[hl] = jnp.exp(lg * (pos + 1.0))
            zeta_ref[hl] = jnp.exp(lg * (ch - 1.0 - pos))

    for hl in range(hp):
        q = q_ref[:, hl * hd:(hl + 1) * hd]
        k = k_ref[:, hl * hd:(hl + 1) * hd]
        v = v_ref[:, hl * hd:(hl + 1) * hd]
        s_prev = s_ref[hl]
        gamma_chunk = jnp.exp(_log_sigmoid(lg_ref[hl])[:, 0:1] * ch)
        qk = lax.dot_general(q, k, (((1,), (1,)), ((), ())), preferred_element_type=F32)
        sc = (qk * dec_ref[hl]).astype(BF16)
        o_ref[:, hl * hd:(hl + 1) * hd] = _dot(sc, v) + _dot(q, s_prev.astype(BF16)) * xi_ref[hl]
        kz = (k.astype(F32) * zeta_ref[hl]).astype(BF16)
        s_ref[hl] = gamma_chunk * s_prev + lax.dot_general(kz, v, (((0,), (0,)), ((), ())),
                                                           preferred_element_type=F32)


def _ret_scan(q, k, v, decay_logit, n_heads, hd):
    m = q.shape[0]
    ch = SCAN_CHUNK
    n_blk = m // ch
    blk = _scan_block(n_blk)
    hp = SCAN_HEADS_PER_STEP
    hb = n_heads // hp
    lg = jnp.broadcast_to(decay_logit.astype(F32).reshape(2 * n_heads, 1, 1), (2 * n_heads, 1, LANES))
    return pl.pallas_call(
        functools.partial(_ret_kernel, hp=hp),
        grid=(2, hb, n_blk),
        in_specs=[
            pl.BlockSpec((ch, hp * hd), lambda d, h, c: (blk(d, c), h)),
            pl.BlockSpec((ch, hp * hd), lambda d, h, c: (blk(d, c), h)),
            pl.BlockSpec((ch, hp * hd), lambda d, h, c: (blk(d, c), h)),
            pl.BlockSpec((hp, 1, LANES), lambda d, h, c: (d * hb + h, 0, 0)),
        ],
        out_specs=pl.BlockSpec((None, ch, hp * hd), lambda d, h, c: (d, blk(d, c), h)),
        out_shape=jax.ShapeDtypeStruct((2, m, n_heads * hd), F32),
        scratch_shapes=[
            pltpu.VMEM((hp, hd, hd), F32),
            pltpu.VMEM((hp, ch, ch), F32),
            pltpu.VMEM((hp, ch, hd), F32),
            pltpu.VMEM((hp, ch, hd), F32),
        ],
        compiler_params=_params(("parallel", "parallel", "arbitrary")),
        name="ret_scan",
    )(q, k, v, lg)


def _split_pairs(w, n_heads, hd):
    lead = w.shape[:-1]
    return w.reshape(*lead, n_heads, hd // 2, 2).swapaxes(-1, -2).reshape(*lead, n_heads * hd)


def _rope_tables(n_lat, n_ctx, head_dim):
    t = jnp.arange(n_lat)
    row = (t // GRID_W).astype(F32)
    col = (t % GRID_W).astype(F32)
    axis_dim = head_dim // 2
    inv_freq = 1.0 / (ROPE_THETA ** (jnp.arange(0, axis_dim, 2, dtype=F32) / axis_dim))
    ang = jnp.concatenate([row[:, None] * inv_freq, col[:, None] * inv_freq], axis=-1)
    cos = jnp.concatenate([jnp.ones((n_ctx, head_dim // 2), F32), jnp.cos(ang)], axis=0)
    sin = jnp.concatenate([jnp.zeros((n_ctx, head_dim // 2), F32), jnp.sin(ang)], axis=0)
    return jnp.concatenate([cos, cos], axis=1), jnp.concatenate([-sin, sin], axis=1)


def kernel(x, c, ctx, c_ctx, ada_w, ada_b, norm1_g, norm2_g, ffn_w_up, ffn_conv_w, ffn_conv_b, ffn_w_down, pool_w, pool_scale, attn_wq, attn_wk, attn_wv, attn_wo, attn_q_norm, attn_k_norm, mlstm_wq, mlstm_wk, mlstm_wv, mlstm_w_gates, mlstm_b_gates, mlstm_w_ogate, mlstm_out_norm, mlstm_wo, ret_wq, ret_wk, ret_wv, ret_wg, ret_decay_logit, ret_out_norm, ret_wo):
    batch, n_lat, d = x.shape
    n_ctx = ctx.shape[1]
    depth = ada_w.shape[0]
    assert batch == 1 and n_ctx % BF16_SUBLANES == 0 and n_lat % n_ctx == 0

    z = jnp.concatenate([ctx[0], x[0]], axis=0)
    cc = jnp.zeros((8, d), F32).at[0].set(c[0]).at[1].set(c_ctx)
    mods = _ada_mods(cc, ada_w, ada_b)
    w_up = ffn_w_up.astype(BF16)
    w_down = ffn_w_down.astype(BF16)

    for i in range(depth):
        kind, j = i % 4, i // 4
        mod = mods[i, :2]
        if kind == 0:
            pooled = _pool_pre(z, norm1_g[i], mod, n_ctx)
            z = _mm_resid(pooled, pool_w[j].astype(BF16), z, mod, 2, n_ctx, colscale=pool_scale[j],
                          groups=len(POOL_WINDOWS), name="pool_mix")
        elif kind == 1:
            hd = ATTN_HEAD_DIM
            qh = attn_wq.shape[2] // hd
            kvh = attn_wk.shape[2] // hd
            hn = _norm_mod(z, norm1_g[i], mod, 0, n_ctx)
            cos, sin = _rope_tables(n_lat, n_ctx, hd)
            gq = jnp.tile(_split_pairs(attn_q_norm[j], 1, hd) * (hd ** -0.5 * LOG2_E), qh)
            gk = jnp.tile(_split_pairs(attn_k_norm[j], 1, hd), kvh)
            q = _head_proj(hn, _pair_cast(attn_wq[j], hd), gq, cos, sin, hd, True, "attn_q")
            k = _head_proj(hn, _pair_cast(attn_wk[j], hd), gk, cos, sin, hd, True, "attn_k")
            v = _mm_plain(hn, attn_wv[j].astype(BF16), name="attn_v")
            o = _flash(q, k, v, n_ctx)
            z = _mm_resid(o, attn_wo[j].astype(BF16), z, mod, 2, n_ctx, name="attn_out")
        elif kind == 2:
            nh = MLSTM_HEADS
            dk = mlstm_wq.shape[2] // nh
            dv = mlstm_wv.shape[2] // nh
            hn = _norm_mod(z, norm1_g[i], mod, 0, n_ctx)
            q = _mm_plain(hn, mlstm_wq[j].astype(BF16), name="mlstm_q")
            k = _mm_plain(hn, (mlstm_wk[j] * dk ** -0.5).astype(BF16), name="mlstm_k")
            v = _mm_plain(hn, mlstm_wv[j].astype(BF16), name="mlstm_v")
            wg = jnp.zeros((d, LANES), F32).at[:, :4 * nh].set(mlstm_w_gates[j]).astype(BF16)
            bg = jnp.zeros((LANES,), F32).at[:4 * nh].set(mlstm_b_gates[j])
            g2 = _gate_prep(_mm_bias(hn, wg, bg, name="mlstm_gates"), nh)
            g2t = g2.T.reshape(LANES, 1, -1)
            hfb = _mlstm_scan(q, k, v, g2, g2t, nh, dk, dv)
            gated = _mm_gated_norm(hn, mlstm_w_ogate[j].astype(BF16), hfb, mlstm_out_norm[j], dv,
                                   _sigmoid, "mlstm_ogate")
            z = _mm_resid(gated, mlstm_wo[j].astype(BF16), z, mod, 2, n_ctx, name="mlstm_out")
        else:
            nh = RET_HEADS
            hd = ret_wq.shape[2] // nh
            hn = _norm_mod(z, norm1_g[i], mod, 0, n_ctx)
            cos, sin = _rope_tables(n_lat, n_ctx, hd)
            ones = jnp.ones((nh * hd,), F32)
            q = _head_proj(hn, _pair_cast(ret_wq[j], hd), ones, cos, sin, hd, False, "ret_q")
            k = _head_proj(hn, _pair_cast(ret_wk[j], hd, hd ** -0.5), ones, cos, sin, hd, False, "ret_k")
            v = _mm_plain(hn, ret_wv[j].astype(BF16), name="ret_v")
            ofb = _ret_scan(q, k, v, ret_decay_logit[j], nh, hd)
            gated = _mm_gated_norm(hn, ret_wg[j].astype(BF16), ofb, ret_out_norm[j], hd, _silu, "ret_gate")
            z = _mm_resid(gated, ret_wo[j].astype(BF16), z, mod, 2, n_ctx, name="ret_out")
        z = _conv_ffn(z, norm2_g[i], mod, w_up, w_down, i, ffn_conv_w[i], ffn_conv_b[i], n_ctx)
    return z[n_ctx:][None]
```

```python
import functools

import jax
import jax.numpy as jnp
from jax import lax
from jax.experimental import pallas as pl
from jax.experimental.pallas import tpu as pltpu

F32 = jnp.float32
BF16 = jnp.bfloat16

EPS = 1e-6
GRID_W = 64
ROPE_THETA = 10000.0
N_MOD = 6
POOL_WINDOWS = (2, 4, 8, 16)
ATTN_HEAD_DIM = 128
ATTN_GROUP = 4
MLSTM_HEADS = 8
RET_HEADS = 16
CONV_WIDTH = 3
LOG2_E = 1.4426950408889634

VMEM_LIMIT_BYTES = 56 * 1024 * 1024
BF16_SUBLANES = 16
F32_SUBLANES = 8
LANES = 128
WIDE_TILE_MAX_K = 4096
FLASH_KEY_CHUNK = 1280
SCAN_CHUNK = 256
SCAN_HEADS_PER_STEP = 4
NORM_UNROLL = 4
POOL_HALO = 8
POOL_ROWS = 32


def _pick(n, candidates):
    for c in candidates:
        if c <= n and n % c == 0:
            return c
    return n


def _params(sem):
    return pltpu.CompilerParams(dimension_semantics=sem, vmem_limit_bytes=VMEM_LIMIT_BYTES)


def _silu(x):
    return x * (1.0 / (1.0 + jnp.exp(-x)))


def _sigmoid(x):
    return 1.0 / (1.0 + jnp.exp(-x))


def _log_sigmoid(x):
    return jnp.minimum(x, 0.0) - jnp.log(1.0 + jnp.exp(-jnp.abs(x)))


def _row_ids(i, tm):
    return i * tm + lax.broadcasted_iota(jnp.int32, (tm, 1), 0)


def _sel_rows(is_ctx, ref2):
    return jnp.where(is_ctx, ref2[1:2, :], ref2[0:1, :])


def _dot(a, b):
    return jnp.dot(a, b, preferred_element_type=F32)


def _ada_kernel(a_ref, w_ref, b_ref, o_ref):
    a = _silu(a_ref[...]).astype(BF16)
    o_ref[...] = _dot(a, w_ref[...].astype(BF16)) + b_ref[...]


def _ada_mods(cc, ada_w, ada_b):
    depth, d, n = ada_w.shape
    tn = _pick(n, (512, 256, 128))
    rows = cc.shape[0]
    return pl.pallas_call(
        _ada_kernel,
        grid=(depth, n // tn),
        in_specs=[
            pl.BlockSpec((rows, d), lambda l, j: (0, 0)),
            pl.BlockSpec((None, d, tn), lambda l, j: (l, 0, j)),
            pl.BlockSpec((None, 1, tn), lambda l, j: (l, 0, j)),
        ],
        out_specs=pl.BlockSpec((None, rows, tn), lambda l, j: (l, 0, j)),
        out_shape=jax.ShapeDtypeStruct((depth, rows, n), F32),
        compiler_params=_params(("parallel", "parallel")),
        name="ada_mods",
    )(cc, ada_w, ada_b.reshape(depth, 1, n))


def _norm_kernel(x_ref, g_ref, sh_ref, sc_ref, o_ref, a_ref):
    a_ref[...] = g_ref[...] * (1.0 + sc_ref[...])
    s8 = F32_SUBLANES

    def body(r, carry):
        base = pl.multiple_of(r * BF16_SUBLANES, BF16_SUBLANES)
        ys = []
        for part in range(BF16_SUBLANES // s8):
            x = x_ref[pl.ds(pl.multiple_of(base + part * s8, s8), s8), :]
            ms = jnp.mean(x * x, axis=-1, keepdims=True)
            ys.append(x * lax.rsqrt(ms + EPS) * a_ref[...] + sh_ref[...])
        o_ref[pl.ds(base, BF16_SUBLANES), :] = jnp.concatenate(ys, axis=0).astype(o_ref.dtype)
        return carry

    lax.fori_loop(0, x_ref.shape[0] // BF16_SUBLANES, body, 0, unroll=NORM_UNROLL)


def _norm_mod(z, g, mod, k_shift, n_ctx):
    m, d = z.shape
    s8 = F32_SUBLANES
    tm = _pick(n_ctx, (256, 128, 64, 32, 16))
    mod8 = jnp.repeat(mod, s8, axis=0)
    vec_blk = lambda i: jnp.where(i * tm < n_ctx, 1, 0)
    return pl.pallas_call(
        _norm_kernel,
        grid=(m // tm,),
        in_specs=[
            pl.BlockSpec((tm, d), lambda i: (i, 0)),
            pl.BlockSpec((s8, d), lambda i: (0, 0)),
            pl.BlockSpec((s8, d), lambda i: (vec_blk(i), k_shift)),
            pl.BlockSpec((s8, d), lambda i: (vec_blk(i), k_shift + 1)),
        ],
        out_specs=pl.BlockSpec((tm, d), lambda i: (i, 0)),
        out_shape=jax.ShapeDtypeStruct((m, d), BF16),
        scratch_shapes=[pltpu.VMEM((s8, d), F32)],
        compiler_params=_params(("parallel",)),
        name="norm_mod",
    )(z, jnp.broadcast_to(g.reshape(1, d), (s8, d)), mod8, mod8)


def _mm_tiles(m, n, n_ctx, wide=False):
    tm = _pick(m, (768, 512, 256, 128))
    tn = _pick(n, (1024, 512, 256, 128) if wide else (512, 256, 128))
    return tm, tn


def _resid_kernel(a_ref, w_ref, x_ref, g_ref, cs_ref, o_ref, *, tm, n_ctx):
    is_ctx = _row_ids(pl.program_id(0), tm) < n_ctx
    acc = _dot(a_ref[...], w_ref[...]) * cs_ref[...]
    o_ref[...] = x_ref[...] + _sel_rows(is_ctx, g_ref) * acc


def _mm_resid(a, w, x, mod, k_gate, n_ctx, colscale=None, groups=1, layer=None, name="mm_resid"):
    m, ka = a.shape
    n = x.shape[1]
    kg = ka // groups
    ng = n // groups
    tm, tn = _mm_tiles(m, ng, n_ctx, wide=kg <= WIDE_TILE_MAX_K)
    jn = ng // tn
    if colscale is None:
        colscale = jnp.ones((n,), F32)
    if layer is not None:
        w_spec = pl.BlockSpec((None, kg, tn), lambda i, j: (layer, 0, j))
    elif groups == 1:
        w_spec = pl.BlockSpec((kg, tn), lambda i, j: (0, j))
    else:
        w_spec = pl.BlockSpec((None, kg, tn), lambda i, j: (j // jn, 0, j % jn))
    return pl.pallas_call(
        functools.partial(_resid_kernel, tm=tm, n_ctx=n_ctx),
        grid=(m // tm, n // tn),
        in_specs=[
            pl.BlockSpec((tm, kg), lambda i, j: (i, j // jn)),
            w_spec,
            pl.BlockSpec((tm, tn), lambda i, j: (i, j)),
            pl.BlockSpec((2, tn), lambda i, j: (0, k_gate * (n // tn) + j)),
            pl.BlockSpec((1, tn), lambda i, j: (0, j)),
        ],
        out_specs=pl.BlockSpec((tm, tn), lambda i, j: (i, j)),
        out_shape=jax.ShapeDtypeStruct((m, n), F32),
        compiler_params=_params(("parallel", "parallel")),
        name=name,
    )(a, w, x, mod, colscale.reshape(1, n))


def _bias_kernel(a_ref, w_ref, b_ref, o_ref):
    o_ref[...] = _dot(a_ref[...], w_ref[...]) + b_ref[...]


def _mm_bias(a, w, b, name="mm_bias"):
    m, k = a.shape
    n = w.shape[1]
    tm, tn = _mm_tiles(m, n, 0)
    return pl.pallas_call(
        _bias_kernel,
        grid=(m // tm, n // tn),
        in_specs=[
            pl.BlockSpec((tm, k), lambda i, j: (i, 0)),
            pl.BlockSpec((k, tn), lambda i, j: (0, j)),
            pl.BlockSpec((1, tn), lambda i, j: (0, j)),
        ],
        out_specs=pl.BlockSpec((tm, tn), lambda i, j: (i, j)),
        out_shape=jax.ShapeDtypeStruct((m, n), F32),
        compiler_params=_params(("parallel", "parallel")),
        name=name,
    )(a, w, b.reshape(1, n))


def _plain_kernel(a_ref, w_ref, o_ref):
    o_ref[...] = _dot(a_ref[...], w_ref[...]).astype(o_ref.dtype)


def _mm_plain(a, w, name="mm_plain"):
    m, k = a.shape
    n = w.shape[1]
    tm, tn = _mm_tiles(m, n, 0, wide=True)
    return pl.pallas_call(
        _plain_kernel,
        grid=(m // tm, n // tn),
        in_specs=[
            pl.BlockSpec((tm, k), lambda i, j: (i, 0)),
            pl.BlockSpec((k, tn), lambda i, j: (0, j)),
        ],
        out_specs=pl.BlockSpec((tm, tn), lambda i, j: (i, j)),
        out_shape=jax.ShapeDtypeStruct((m, n), BF16),
        compiler_params=_params(("parallel", "parallel")),
        name=name,
    )(a, w)


def _ffn_up_kernel(a_ref, ap_ref, an_ref, wg_ref, wv_ref, cwg_ref, cwv_ref, cbg_ref, cbv_ref,
                   o_ref, ext_ref, *, tm, n_ctx, m_total):
    i = pl.program_id(0)
    h = BF16_SUBLANES

    @pl.when(pl.program_id(1) == 0)
    def _():
        ext_ref[0:h, :] = ap_ref[...]
        ext_ref[h:h + tm, :] = a_ref[...]
        ext_ref[h + tm:h + tm + h, :] = an_ref[...]

    rows = _row_ids(i, tm)
    seq_first = (rows == 0) | (rows == n_ctx)
    seq_last = (rows == n_ctx - 1) | (rows == m_total - 1)
    a_ext = ext_ref[...]
    n_ext = tm + 2 * h

    def conv(w_ref, cw_ref, cb_ref):
        u = _dot(a_ext, w_ref[...])
        u_prev = pltpu.roll(u, 1, axis=0)[h:h + tm]
        u_next = pltpu.roll(u, n_ext - 1, axis=0)[h:h + tm]
        u_prev = jnp.where(seq_first, 0.0, u_prev)
        u_next = jnp.where(seq_last, 0.0, u_next)
        cw = cw_ref[...]
        return u_prev * cw[0:1] + u[h:h + tm] * cw[1:2] + u_next * cw[2:3] + cb_ref[...]

    gate = conv(wg_ref, cwg_ref, cbg_ref)
    val = conv(wv_ref, cwv_ref, cbv_ref)
    o_ref[...] = (_silu(gate) * val).astype(o_ref.dtype)


def _ffn_up(hn, w_up, layer, conv_w, conv_b, n_ctx):
    m, d = hn.shape
    dff = w_up.shape[2] // 2
    tm, tn = _mm_tiles(m, dff, n_ctx)
    jn = dff // tn
    h = BF16_SUBLANES
    rb = tm // h
    last_rb = m // h - 1
    return pl.pallas_call(
        functools.partial(_ffn_up_kernel, tm=tm, n_ctx=n_ctx, m_total=m),
        grid=(m // tm, jn),
        in_specs=[
            pl.BlockSpec((tm, d), lambda i, j: (i, 0)),
            pl.BlockSpec((h, d), lambda i, j: (jnp.maximum(i * rb - 1, 0), 0)),
            pl.BlockSpec((h, d), lambda i, j: (jnp.minimum((i + 1) * rb, last_rb), 0)),
            pl.BlockSpec((None, d, tn), lambda i, j: (layer, 0, j)),
            pl.BlockSpec((None, d, tn), lambda i, j: (layer, 0, jn + j)),
            pl.BlockSpec((CONV_WIDTH, tn), lambda i, j: (0, j)),
            pl.BlockSpec((CONV_WIDTH, tn), lambda i, j: (0, jn + j)),
            pl.BlockSpec((1, tn), lambda i, j: (0, j)),
            pl.BlockSpec((1, tn), lambda i, j: (0, jn + j)),
        ],
        out_specs=pl.BlockSpec((tm, tn), lambda i, j: (i, j)),
        out_shape=jax.ShapeDtypeStruct((m, dff), BF16),
        scratch_shapes=[pltpu.VMEM((tm + 2 * h, d), BF16)],
        compiler_params=_params(("parallel", "arbitrary")),
        name="ffn_up",
    )(hn, hn, hn, w_up, w_up, conv_w, conv_w, conv_b.reshape(1, -1), conv_b.reshape(1, -1))


def _conv_ffn(z, norm_g, mod, w_up, w_down, layer, conv_w, conv_b, n_ctx):
    hn = _norm_mod(z, norm_g, mod, 3, n_ctx)
    act = _ffn_up(hn, w_up, layer, conv_w, conv_b, n_ctx)
    return _mm_resid(act, w_down, z, mod, 5, n_ctx, layer=layer, name="ffn_down")


def _pool_kernel(x_ref, xp_ref, xn_ref, g_ref, sh_ref, sc_ref, o_ref, s_ref, a_ref, *, tm, n_ctx, m_total):
    i = pl.program_id(0)
    p = POOL_HALO
    d = x_ref.shape[1]
    cg = d // len(POOL_WINDOWS)
    a_ref[...] = g_ref[...] * (1.0 + sc_ref[...])

    def modulated(x):
        ms = jnp.mean(x * x, axis=-1, keepdims=True)
        return x * lax.rsqrt(ms + EPS) * a_ref[...] + sh_ref[...]

    start = i * tm
    stop = start + tm
    prev_ok = jnp.logical_and(start != 0, start != n_ctx)
    next_ok = jnp.logical_and(stop != n_ctx, stop != m_total)
    s_ref[0:p, :] = jnp.where(prev_ok, modulated(xp_ref[...]), 0.0)
    s_ref[p + tm:p + tm + p, :] = jnp.where(next_ok, modulated(xn_ref[...]), 0.0)

    def body(r, carry):
        base = pl.multiple_of(r * p, p)
        s_ref[pl.ds(p + base, p), :] = modulated(x_ref[pl.ds(base, p), :])
        return carry

    lax.fori_loop(0, tm // p, body, 0, unroll=NORM_UNROLL)

    in_ctx = start < n_ctx
    t0 = jnp.where(in_ctx, start, start - n_ctx)
    t_len = jnp.where(in_ctx, n_ctx, m_total - n_ctx)
    nr = POOL_ROWS
    for rb in range(0, tm, nr):
        t = t0 + rb + lax.broadcasted_iota(jnp.int32, (nr, 1), 0)
        for gi, w in enumerate(POOL_WINDOWS):
            cols = slice(gi * cg, (gi + 1) * cg)
            acc = s_ref[p + rb - w // 2:p + rb - w // 2 + nr, cols]
            for k in range(1 - w // 2, w - w // 2):
                acc = acc + s_ref[p + rb + k:p + rb + k + nr, cols]
            lo = jnp.maximum(t - w // 2, 0)
            hi = jnp.minimum(t + w - w // 2, t_len)
            cnt = (hi - lo).astype(F32)
            o_ref[rb:rb + nr, cols] = (acc / cnt - s_ref[p + rb:p + rb + nr, cols]).astype(o_ref.dtype)


def _pool_pre(z, g, mod, n_ctx):
    m, d = z.shape
    tm = _pick(n_ctx, (256, 128, 64, 32))
    p = POOL_HALO
    rb = tm // p
    last_rb = m // p - 1
    mod8 = jnp.repeat(mod, p, axis=0)
    vec_blk = lambda i: jnp.where(i * tm < n_ctx, 1, 0)
    return pl.pallas_call(
        functools.partial(_pool_kernel, tm=tm, n_ctx=n_ctx, m_total=m),
        grid=(m // tm,),
        in_specs=[
            pl.BlockSpec((tm, d), lambda i: (i, 0)),
            pl.BlockSpec((p, d), lambda i: (jnp.maximum(i * rb - 1, 0), 0)),
            pl.BlockSpec((p, d), lambda i: (jnp.minimum((i + 1) * rb, last_rb), 0)),
            pl.BlockSpec((p, d), lambda i: (0, 0)),
            pl.BlockSpec((p, d), lambda i: (vec_blk(i), 0)),
            pl.BlockSpec((p, d), lambda i: (vec_blk(i), 1)),
        ],
        out_specs=pl.BlockSpec((tm, d), lambda i: (i, 0)),
        out_shape=jax.ShapeDtypeStruct((m, d), BF16),
        scratch_shapes=[pltpu.VMEM((tm + 2 * p, d), F32), pltpu.VMEM((p, d), F32)],
        compiler_params=_params(("parallel",)),
        name="pool_pre",
    )(z, z, z, jnp.broadcast_to(g.reshape(1, d), (p, d)), mod8, mod8)


def _pair_cast_kernel(w_ref, o_ref, *, hd, scale):
    r = lax.broadcasted_iota(jnp.int32, (hd, hd), 0)
    c = lax.broadcasted_iota(jnp.int32, (hd, hd), 1)
    src_col = jnp.where(c < hd // 2, 2 * c, 2 * (c - hd // 2) + 1)
    perm = jnp.where(r == src_col, 1.0, 0.0).astype(BF16)
    for hh in range(w_ref.shape[1] // hd):
        cols = slice(hh * hd, (hh + 1) * hd)
        wh = (w_ref[:, cols] * scale).astype(BF16)
        o_ref[:, cols] = _dot(wh, perm).astype(BF16)


def _pair_cast(w, hd, scale=1.0):
    k, n = w.shape
    tk = _pick(k, (1024, 512, 256, 128))
    tn = _pick(n, (1024, 512, 256, 128))
    return pl.pallas_call(
        functools.partial(_pair_cast_kernel, hd=hd, scale=scale),
        grid=(k // tk, n // tn),
        in_specs=[pl.BlockSpec((tk, tn), lambda i, j: (i, j))],
        out_specs=pl.BlockSpec((tk, tn), lambda i, j: (i, j)),
        out_shape=jax.ShapeDtypeStruct((k, n), BF16),
        compiler_params=_params(("parallel", "parallel")),
        name="pair_cast",
    )(w)


def _head_proj_kernel(a_ref, w_ref, g_ref, c_ref, s_ref, o_ref, *, hd, normalize):
    acc = _dot(a_ref[...], w_ref[...])
    cos = c_ref[...]
    sin = s_ref[...]
    mean_w = jnp.full((2 * hd, hd), 1.0 / hd, BF16)
    for hh in range(acc.shape[1] // hd):
        cols = slice(hh * hd, (hh + 1) * hd)
        y = acc[:, cols]
        if normalize:
            sq = y * y
            hi = sq.astype(BF16)
            lo = (sq - hi.astype(F32)).astype(BF16)
            ms = _dot(jnp.concatenate([hi, lo], axis=1), mean_w)
            y = y * lax.rsqrt(ms + EPS) * g_ref[:, cols]
        o_ref[:, cols] = (y * cos + pltpu.roll(y, hd // 2, axis=1) * sin).astype(o_ref.dtype)


def _head_proj(hn, w, gain, cos, sin, hd, normalize, name):
    m, d = hn.shape
    n = w.shape[1]
    tm, tn = _mm_tiles(m, n, 0, wide=True)
    tn = max(tn, hd)
    return pl.pallas_call(
        functools.partial(_head_proj_kernel, hd=hd, normalize=normalize),
        grid=(m // tm, n // tn),
        in_specs=[
            pl.BlockSpec((tm, d), lambda i, j: (i, 0)),
            pl.BlockSpec((d, tn), lambda i, j: (0, j)),
            pl.BlockSpec((1, tn), lambda i, j: (0, j)),
            pl.BlockSpec((tm, hd), lambda i, j: (i, 0)),
            pl.BlockSpec((tm, hd), lambda i, j: (i, 0)),
        ],
        out_specs=pl.BlockSpec((tm, tn), lambda i, j: (i, j)),
        out_shape=jax.ShapeDtypeStruct((m, n), BF16),
        compiler_params=_params(("parallel", "parallel")),
        name=name,
    )(hn, w, gain.reshape(1, n), cos, sin)


def _flash_kernel(q_ref, kt_ref, v_ref, o_ref, qs_ref, vx_ref, *, tq, chunk):
    hd = ATTN_HEAD_DIM
    n_keys = v_ref.shape[0]

    @pl.when(pl.program_id(1) == 0)
    def _():
        vx_ref[:, 0:hd] = v_ref[...]
        lane = lax.broadcasted_iota(jnp.int32, (n_keys, hd), 1)
        vx_ref[:, hd:2 * hd] = jnp.where(lane == 0, 1.0, 0.0).astype(BF16)

    for gi in range(ATTN_GROUP):
        qs_ref[gi * tq:(gi + 1) * tq, :] = q_ref[:, gi * hd:(gi + 1) * hd]
    q = qs_ref[...]
    m_run = None
    acc = None
    for start in range(0, n_keys, chunk):
        size = min(chunk, n_keys - start)
        s = _dot(q, kt_ref[:, start:start + size])
        m_c = jnp.max(s, axis=-1, keepdims=True)
        m_new = m_c if m_run is None else jnp.maximum(m_run, m_c)
        p = jnp.exp2(s - m_new).astype(BF16)
        pv = _dot(p, vx_ref[start:start + size, :])
        acc = pv if acc is None else jnp.exp2(m_run - m_new) * acc + pv
        m_run = m_new
    out = acc[:, 0:hd] / acc[:, hd:hd + 1]
    for gi in range(ATTN_GROUP):
        o_ref[:, gi * hd:(gi + 1) * hd] = out[gi * tq:(gi + 1) * tq].astype(o_ref.dtype)


def _flash_part(q, kt, v, row0, n_rows, n_keys):
    hd = ATTN_HEAD_DIM
    kvh = v.shape[1] // hd
    tq = _pick(n_rows, (256, 128, 64, 32, 16))
    chunk = min(n_keys, FLASH_KEY_CHUNK)
    gw = ATTN_GROUP * hd
    r0 = row0 // tq
    return pl.pallas_call(
        functools.partial(_flash_kernel, tq=tq, chunk=chunk),
        grid=(kvh, n_rows // tq),
        in_specs=[
            pl.BlockSpec((tq, gw), lambda g, i: (r0 + i, g)),
            pl.BlockSpec((hd, n_keys), lambda g, i: (g, 0)),
            pl.BlockSpec((n_keys, hd), lambda g, i: (0, g)),
        ],
        out_specs=pl.BlockSpec((tq, gw), lambda g, i: (i, g)),
        out_shape=jax.ShapeDtypeStruct((n_rows, q.shape[1]), BF16),
        scratch_shapes=[
            pltpu.VMEM((ATTN_GROUP * tq, hd), BF16),
            pltpu.VMEM((n_keys, 2 * hd), BF16),
        ],
        compiler_params=_params(("parallel", "arbitrary")),
        name="flash_attn",
    )(q, kt, v)


def _flash(q, k, v, n_ctx):
    m = q.shape[0]
    kt = k.T
    o_lat = _flash_part(q, kt, v, n_ctx, m - n_ctx, m)
    o_ctx = _flash_part(q, kt, v, 0, n_ctx, n_ctx)
    return jnp.concatenate([o_ctx, o_lat], axis=0)


def _gate_prep_kernel(x_ref, o_ref, *, n_heads):
    x = x_ref[...]
    rows = x.shape[0]
    col = lax.broadcasted_iota(jnp.int32, x.shape, 1)
    is_f = jnp.logical_and((col // n_heads) % 2 == 1, col < 4 * n_heads)
    backward = col >= 2 * n_heads
    logf = _log_sigmoid(x)
    r = lax.broadcasted_iota(jnp.int32, (rows, rows), 0)
    c = lax.broadcasted_iota(jnp.int32, (rows, rows), 1)
    tril = (c <= r).astype(F32)
    triu = (c >= r).astype(F32)
    fwd = jnp.dot(tril, logf, preferred_element_type=F32, precision=lax.Precision.HIGHEST)
    bwd = jnp.dot(triu, logf, preferred_element_type=F32, precision=lax.Precision.HIGHEST)
    o_ref[...] = jnp.where(is_f, jnp.where(backward, bwd, fwd), x)


def _gate_prep(gates, n_heads):
    m, n = gates.shape
    ch = SCAN_CHUNK
    return pl.pallas_call(
        functools.partial(_gate_prep_kernel, n_heads=n_heads),
        grid=(m // ch,),
        in_specs=[pl.BlockSpec((ch, n), lambda c: (c, 0))],
        out_specs=pl.BlockSpec((ch, n), lambda c: (c, 0)),
        out_shape=jax.ShapeDtypeStruct((m, n), F32),
        compiler_params=_params(("parallel",)),
        name="gate_prep",
    )(gates)


def _scan_block(n_blk):
    def blk(d, c):
        return jnp.where(c == 0, 0, jnp.where(d == 0, c, n_blk - c))
    return blk


def _mlstm_kernel(q_ref, k_ref, v_ref, gc_ref, ir_ref, br_ref, o_ref, s_ref, n_ref, m_ref, *, n_heads, hp):
    d = pl.program_id(0)
    ch = q_ref.shape[0]
    dk = q_ref.shape[1] // hp
    dv = v_ref.shape[1] // hp

    @pl.when(pl.program_id(2) == 0)
    def _():
        s_ref[...] = jnp.zeros(s_ref.shape, F32)
        n_ref[...] = jnp.zeros(n_ref.shape, F32)
        m_ref[...] = jnp.zeros(m_ref.shape, F32)

    gc = gc_ref[...]
    lane = lax.broadcasted_iota(jnp.int32, gc.shape, 1)
    r = lax.broadcasted_iota(jnp.int32, (ch, ch), 0)
    c = lax.broadcasted_iota(jnp.int32, (ch, ch), 1)
    sgn = 1 - 2 * d
    seen = (r - c) * sgn >= 0

    for hl in range(hp):
        q = q_ref[:, hl * dk:(hl + 1) * dk]
        k = k_ref[:, hl * dk:(hl + 1) * dk]
        v = v_ref[:, hl * dv:(hl + 1) * dv]
        base = d * 2 * n_heads + pl.program_id(1) * hp + hl
        i_col = jnp.sum(jnp.where(lane == base, gc, 0.0), axis=1, keepdims=True)
        b_col = jnp.sum(jnp.where(lane == base + n_heads, gc, 0.0), axis=1, keepdims=True)
        i_row = ir_ref[hl]
        b_row = br_ref[hl]
        m_prev = m_ref[hl]
        s_prev = s_ref[hl]
        n_prev = n_ref[hl]

        log_d = jnp.where(seen, b_col - b_row + i_row, -jnp.inf)
        log_prev = b_col + m_prev
        m_t = jnp.maximum(log_prev, jnp.max(log_d, axis=-1, keepdims=True))
        qk = lax.dot_general(q, k, (((1,), (1,)), ((), ())), preferred_element_type=F32)
        s = qk * jnp.exp(log_d - m_t)
        w_prev = jnp.exp(log_prev - m_t)
        num = _dot(s.astype(BF16), v) + w_prev * _dot(q, s_prev.astype(BF16))
        den = (jnp.sum(s, axis=-1, keepdims=True)
               + w_prev * jnp.sum(q.astype(F32) * n_prev, axis=-1, keepdims=True))
        o_ref[:, hl * dv:(hl + 1) * dv] = num / jnp.maximum(jnp.abs(den), jnp.exp(-m_t))

        b_last = jnp.where(d == 0, b_row[:, ch - 1:ch], b_row[:, 0:1])
        log_w_row = b_last - b_row + i_row
        m_new = jnp.maximum(b_last + m_prev, jnp.max(log_w_row, axis=-1, keepdims=True))
        w_col = jnp.exp(b_last - b_col + i_col - m_new)
        decay = jnp.exp(b_last + m_prev - m_new)
        wv = (w_col * v.astype(F32)).astype(BF16)
        s_ref[hl] = decay * s_prev + lax.dot_general(k, wv, (((0,), (0,)), ((), ())),
                                                     preferred_element_type=F32)
        n_ref[hl] = decay * n_prev + jnp.sum(w_col * k.astype(F32), axis=0, keepdims=True)
        m_ref[hl] = m_new


def _mlstm_scan(q, k, v, g2, g2t, n_heads, dk, dv):
    m = q.shape[0]
    ch = SCAN_CHUNK
    n_blk = m // ch
    blk = _scan_block(n_blk)
    hp = SCAN_HEADS_PER_STEP
    hb = n_heads // hp
    return pl.pallas_call(
        functools.partial(_mlstm_kernel, n_heads=n_heads, hp=hp),
        grid=(2, hb, n_blk),
        in_specs=[
            pl.BlockSpec((ch, hp * dk), lambda d, h, c: (blk(d, c), h)),
            pl.BlockSpec((ch, hp * dk), lambda d, h, c: (blk(d, c), h)),
            pl.BlockSpec((ch, hp * dv), lambda d, h, c: (blk(d, c), h)),
            pl.BlockSpec((ch, g2.shape[1]), lambda d, h, c: (blk(d, c), 0)),
            pl.BlockSpec((hp, 1, ch), lambda d, h, c: (d * 2 * hb + h, 0, blk(d, c))),
            pl.BlockSpec((hp, 1, ch), lambda d, h, c: (d * 2 * hb + hb + h, 0, blk(d, c))),
        ],
        out_specs=pl.BlockSpec((None, ch, hp * dv), lambda d, h, c: (d, blk(d, c), h)),
        out_shape=jax.ShapeDtypeStruct((2, m, n_heads * dv), F32),
        scratch_shapes=[
            pltpu.VMEM((hp, dk, dv), F32),
            pltpu.VMEM((hp, 1, dk), F32),
            pltpu.VMEM((hp, 1, 1), F32),
        ],
        compiler_params=_params(("parallel", "parallel", "arbitrary")),
        name="mlstm_scan",
    )(q, k, v, g2, g2t, g2t)


def _gated_norm_kernel(a_ref, w_ref, hf_ref, hb_ref, g_ref, o_ref, *, hw, act):
    acc = _dot(a_ref[...], w_ref[...])
    for hh in range(acc.shape[1] // hw):
        cols = slice(hh * hw, (hh + 1) * hw)
        y = hf_ref[:, cols] + hb_ref[:, cols]
        ms = jnp.mean(y * y, axis=-1, keepdims=True)
        y = y * lax.rsqrt(ms + EPS) * g_ref[:, cols]
        o_ref[:, cols] = (act(acc[:, cols]) * y).astype(o_ref.dtype)


def _mm_gated_norm(hn, w, hfb, gain, hw, act, name):
    m, d = hn.shape
    n = w.shape[1]
    tm, tn = _mm_tiles(m, n, 0, wide=True)
    tn = max(tn, hw)
    return pl.pallas_call(
        functools.partial(_gated_norm_kernel, hw=hw, act=act),
        grid=(m // tm, n // tn),
        in_specs=[
            pl.BlockSpec((tm, d), lambda i, j: (i, 0)),
            pl.BlockSpec((d, tn), lambda i, j: (0, j)),
            pl.BlockSpec((None, tm, tn), lambda i, j: (0, i, j)),
            pl.BlockSpec((None, tm, tn), lambda i, j: (1, i, j)),
            pl.BlockSpec((1, tn), lambda i, j: (0, j)),
        ],
        out_specs=pl.BlockSpec((tm, tn), lambda i, j: (i, j)),
        out_shape=jax.ShapeDtypeStruct((m, n), BF16),
        compiler_params=_params(("parallel", "parallel")),
        name=name,
    )(hn, w, hfb, hfb, gain.reshape(1, n))


def _ret_kernel(q_ref, k_ref, v_ref, lg_ref, o_ref, s_ref, dec_ref, xi_ref, zeta_ref, *, hp):
    d = pl.program_id(0)
    ch = q_ref.shape[0]
    hd = q_ref.shape[1] // hp

    @pl.when(pl.program_id(2) == 0)
    def _():
        s_ref[...] = jnp.zeros(s_ref.shape, F32)
        r = lax.broadcasted_iota(jnp.int32, (ch, ch), 0)
        c = lax.broadcasted_iota(jnp.int32, (ch, ch), 1)
        sgn = 1 - 2 * d
        rel = (r - c) * sgn
        seen = rel >= 0
        t = lax.broadcasted_iota(jnp.int32, (ch, hd), 0)
        pos = jnp.where(d == 0, t, ch - 1 - t).astype(F32)
        for hl in range(hp):
            lg = _log_sigmoid(lg_ref[hl])[:, 0:1]
            dec_ref[hl] = jnp.where(seen, jnp.exp(lg * jnp.where(seen, rel, 0).astype(F32)), 0.0)
            xi_ref[hl] = jnp.exp(lg * (pos + 1.0))
            zeta_ref[hl] = jnp.exp(lg * (ch - 1.0 - pos))

    for hl in range(hp):
        q = q_ref[:, hl * hd:(hl + 1) * hd]
        k = k_ref[:, hl * hd:(hl + 1) * hd]
        v = v_ref[:, hl * hd:(hl + 1) * hd]
        s_prev = s_ref[hl]
        gamma_chunk = jnp.exp(_log_sigmoid(lg_ref[hl])[:, 0:1] * ch)
        qk = lax.dot_general(q, k, (((1,), (1,)), ((), ())), preferred_element_type=F32)
        sc = (qk * dec_ref[hl]).astype(BF16)
        o_ref[:, hl * hd:(hl + 1) * hd] = _dot(sc, v) + _dot(q, s_prev.astype(BF16)) * xi_ref[hl]
        kz = (k.astype(F32) * zeta_ref[hl]).astype(BF16)
        s_ref[hl] = gamma_chunk * s_prev + lax.dot_general(kz, v, (((0,), (0,)), ((), ())),
                                                           preferred_element_type=F32)


def _ret_scan(q, k, v, decay_logit, n_heads, hd):
    m = q.shape[0]
    ch = SCAN_CHUNK
    n_blk = m // ch
    blk = _scan_block(n_blk)
    hp = SCAN_HEADS_PER_STEP
    hb = n_heads // hp
    lg = jnp.broadcast_to(decay_logit.astype(F32).reshape(2 * n_heads, 1, 1), (2 * n_heads, 1, LANES))
    return pl.pallas_call(
        functools.partial(_ret_kernel, hp=hp),
        grid=(2, hb, n_blk),
        in_specs=[
            pl.BlockSpec((ch, hp * hd), lambda d, h, c: (blk(d, c), h)),
            pl.BlockSpec((ch, hp * hd), lambda d, h, c: (blk(d, c), h)),
            pl.BlockSpec((ch, hp * hd), lambda d, h, c: (blk(d, c), h)),
            pl.BlockSpec((hp, 1, LANES), lambda d, h, c: (d * hb + h, 0, 0)),
        ],
        out_specs=pl.BlockSpec((None, ch, hp * hd), lambda d, h, c: (d, blk(d, c), h)),
        out_shape=jax.ShapeDtypeStruct((2, m, n_heads * hd), F32),
        scratch_shapes=[
            pltpu.VMEM((hp, hd, hd), F32),
            pltpu.VMEM((hp, ch, ch), F32),
            pltpu.VMEM((hp, ch, hd), F32),
            pltpu.VMEM((hp, ch, hd), F32),
        ],
        compiler_params=_params(("parallel", "parallel", "arbitrary")),
        name="ret_scan",
    )(q, k, v, lg)


def _split_pairs(w, n_heads, hd):
    lead = w.shape[:-1]
    return w.reshape(*lead, n_heads, hd // 2, 2).swapaxes(-1, -2).reshape(*lead, n_heads * hd)


def _rope_tables(n_lat, n_ctx, head_dim):
    t = jnp.arange(n_lat)
    row = (t // GRID_W).astype(F32)
    col = (t % GRID_W).astype(F32)
    axis_dim = head_dim // 2
    inv_freq = 1.0 / (ROPE_THETA ** (jnp.arange(0, axis_dim, 2, dtype=F32) / axis_dim))
    ang = jnp.concatenate([row[:, None] * inv_freq, col[:, None] * inv_freq], axis=-1)
    cos = jnp.concatenate([jnp.ones((n_ctx, head_dim // 2), F32), jnp.cos(ang)], axis=0)
    sin = jnp.concatenate([jnp.zeros((n_ctx, head_dim // 2), F32), jnp.sin(ang)], axis=0)
    return jnp.concatenate([cos, cos], axis=1), jnp.concatenate([-sin, sin], axis=1)


def kernel(x, c, ctx, c_ctx, ada_w, ada_b, norm1_g, norm2_g, ffn_w_up, ffn_conv_w, ffn_conv_b, ffn_w_down, pool_w, pool_scale, attn_wq, attn_wk, attn_wv, attn_wo, attn_q_norm, attn_k_norm, mlstm_wq, mlstm_wk, mlstm_wv, mlstm_w_gates, mlstm_b_gates, mlstm_w_ogate, mlstm_out_norm, mlstm_wo, ret_wq, ret_wk, ret_wv, ret_wg, ret_decay_logit, ret_out_norm, ret_wo):
    batch, n_lat, d = x.shape
    n_ctx = ctx.shape[1]
    depth = ada_w.shape[0]
    assert batch == 1 and n_ctx % BF16_SUBLANES == 0 and n_lat % n_ctx == 0

    z = jnp.concatenate([ctx[0], x[0]], axis=0)
    cc = jnp.zeros((8, d), F32).at[0].set(c[0]).at[1].set(c_ctx)
    mods = _ada_mods(cc, ada_w, ada_b)
    w_up = ffn_w_up.astype(BF16)
    w_down = ffn_w_down.astype(BF16)

    for i in range(depth):
        kind, j = i % 4, i // 4
        mod = mods[i, :2]
        if kind == 0:
            pooled = _pool_pre(z, norm1_g[i], mod, n_ctx)
            z = _mm_resid(pooled, pool_w[j].astype(BF16), z, mod, 2, n_ctx, colscale=pool_scale[j],
                          groups=len(POOL_WINDOWS), name="pool_mix")
        elif kind == 1:
            hd = ATTN_HEAD_DIM
            qh = attn_wq.shape[2] // hd
            kvh = attn_wk.shape[2] // hd
            hn = _norm_mod(z, norm1_g[i], mod, 0, n_ctx)
            cos, sin = _rope_tables(n_lat, n_ctx, hd)
            gq = jnp.tile(_split_pairs(attn_q_norm[j], 1, hd) * (hd ** -0.5 * LOG2_E), qh)
            gk = jnp.tile(_split_pairs(attn_k_norm[j], 1, hd), kvh)
            q = _head_proj(hn, _pair_cast(attn_wq[j], hd), gq, cos, sin, hd, True, "attn_q")
            k = _head_proj(hn, _pair_cast(attn_wk[j], hd), gk, cos, sin, hd, True, "attn_k")
            v = _mm_plain(hn, attn_wv[j].astype(BF16), name="attn_v")
            o = _flash(q, k, v, n_ctx)
            z = _mm_resid(o, attn_wo[j].astype(BF16), z, mod, 2, n_ctx, name="attn_out")
        elif kind == 2:
            nh = MLSTM_HEADS
            dk = mlstm_wq.shape[2] // nh
            dv = mlstm_wv.shape[2] // nh
            hn = _norm_mod(z, norm1_g[i], mod, 0, n_ctx)
            q = _mm_plain(hn, mlstm_wq[j].astype(BF16), name="mlstm_q")
            k = _mm_plain(hn, (mlstm_wk[j] * dk ** -0.5).astype(BF16), name="mlstm_k")
            v = _mm_plain(hn, mlstm_wv[j].astype(BF16), name="mlstm_v")
            wg = jnp.zeros((d, LANES), F32).at[:, :4 * nh].set(mlstm_w_gates[j]).astype(BF16)
            bg = jnp.zeros((LANES,), F32).at[:4 * nh].set(mlstm_b_gates[j])
            g2 = _gate_prep(_mm_bias(hn, wg, bg, name="mlstm_gates"), nh)
            g2t = g2.T.reshape(LANES, 1, -1)
            hfb = _mlstm_scan(q, k, v, g2, g2t, nh, dk, dv)
            gated = _mm_gated_norm(hn, mlstm_w_ogate[j].astype(BF16), hfb, mlstm_out_norm[j], dv,
                                   _sigmoid, "mlstm_ogate")
            z = _mm_resid(gated, mlstm_wo[j].astype(BF16), z, mod, 2, n_ctx, name="mlstm_out")
        else:
            nh = RET_HEADS
            hd = ret_wq.shape[2] // nh
            hn = _norm_mod(z, norm1_g[i], mod, 0, n_ctx)
            cos, sin = _rope_tables(n_lat, n_ctx, hd)
            ones = jnp.ones((nh * hd,), F32)
            q = _head_proj(hn, _pair_cast(ret_wq[j], hd), ones, cos, sin, hd, False, "ret_q")
            k = _head_proj(hn, _pair_cast(ret_wk[j], hd, hd ** -0.5), ones, cos, sin, hd, False, "ret_k")
            v = _mm_plain(hn, ret_wv[j].astype(BF16), name="ret_v")
            ofb = _ret_scan(q, k, v, ret_decay_logit[j], nh, hd)
            gated = _mm_gated_norm(hn, ret_wg[j].astype(BF16), ofb, ret_out_norm[j], hd, _silu, "ret_gate")
            z = _mm_resid(gated, ret_wo[j].astype(BF16), z, mod, 2, n_ctx, name="ret_out")
        z = _conv_ffn(z, norm2_g[i], mod, w_up, w_down, i, ffn_conv_w[i], ffn_conv_b[i], n_ctx)
    return z[n_ctx:][None]
```

```python
import functools

import jax
import jax.numpy as jnp
from jax import lax
from jax.experimental import pallas as pl
from jax.experimental.pallas import tpu as pltpu

F32 = jnp.float32
BF16 = jnp.bfloat16

EPS = 1e-6
GRID_W = 64
ROPE_THETA = 10000.0
N_MOD = 6
POOL_WINDOWS = (2, 4, 8, 16)
ATTN_HEAD_DIM = 128
ATTN_GROUP = 4
MLSTM_HEADS = 8
RET_HEADS = 16
CONV_WIDTH = 3
LOG2_E = 1.4426950408889634

VMEM_LIMIT_BYTES = 56 * 1024 * 1024
BF16_SUBLANES = 16
F32_SUBLANES = 8
LANES = 128
WIDE_TILE_MAX_K = 4096
FLASH_KEY_CHUNK = 1280
SCAN_CHUNK = 256
SCAN_HEADS_PER_STEP = 4
NORM_UNROLL = 4
POOL_HALO = 8
POOL_ROWS = 32


def _pick(n, candidates):
    for c in candidates:
        if c <= n and n % c == 0:
            return c
    return n


def _params(sem):
    return pltpu.CompilerParams(dimension_semantics=sem, vmem_limit_bytes=VMEM_LIMIT_BYTES)


def _silu(x):
    return x * (1.0 / (1.0 + jnp.exp(-x)))


def _sigmoid(x):
    return 1.0 / (1.0 + jnp.exp(-x))


def _log_sigmoid(x):
    return jnp.minimum(x, 0.0) - jnp.log(1.0 + jnp.exp(-jnp.abs(x)))


def _row_ids(i, tm):
    return i * tm + lax.broadcasted_iota(jnp.int32, (tm, 1), 0)


def _sel_rows(is_ctx, ref2):
    return jnp.where(is_ctx, ref2[1:2, :], ref2[0:1, :])


def _dot(a, b):
    return jnp.dot(a, b, preferred_element_type=F32)


def _ada_kernel(a_ref, w_ref, b_ref, o_ref):
    a = _silu(a_ref[...]).astype(BF16)
    o_ref[...] = _dot(a, w_ref[...].astype(BF16)) + b_ref[...]


def _ada_mods(cc, ada_w, ada_b):
    depth, d, n = ada_w.shape
    tn = _pick(n, (512, 256, 128))
    rows = cc.shape[0]
    return pl.pallas_call(
        _ada_kernel,
        grid=(depth, n // tn),
        in_specs=[
            pl.BlockSpec((rows, d), lambda l, j: (0, 0)),
            pl.BlockSpec((None, d, tn), lambda l, j: (l, 0, j)),
            pl.BlockSpec((None, 1, tn), lambda l, j: (l, 0, j)),
        ],
        out_specs=pl.BlockSpec((None, rows, tn), lambda l, j: (l, 0, j)),
        out_shape=jax.ShapeDtypeStruct((depth, rows, n), F32),
        compiler_params=_params(("parallel", "parallel")),
        name="ada_mods",
    )(cc, ada_w, ada_b.reshape(depth, 1, n))


def _norm_kernel(x_ref, g_ref, sh_ref, sc_ref, o_ref, a_ref):
    a_ref[...] = g_ref[...] * (1.0 + sc_ref[...])
    s8 = F32_SUBLANES

    def body(r, carry):
        base = pl.multiple_of(r * BF16_SUBLANES, BF16_SUBLANES)
        ys = []
        for part in range(BF16_SUBLANES // s8):
            x = x_ref[pl.ds(pl.multiple_of(base + part * s8, s8), s8), :]
            ms = jnp.mean(x * x, axis=-1, keepdims=True)
            ys.append(x * lax.rsqrt(ms + EPS) * a_ref[...] + sh_ref[...])
        o_ref[pl.ds(base, BF16_SUBLANES), :] = jnp.concatenate(ys, axis=0).astype(o_ref.dtype)
        return carry

    lax.fori_loop(0, x_ref.shape[0] // BF16_SUBLANES, body, 0, unroll=NORM_UNROLL)


def _norm_mod(z, g, mod, k_shift, n_ctx):
    m, d = z.shape
    s8 = F32_SUBLANES
    tm = _pick(n_ctx, (256, 128, 64, 32, 16))
    mod8 = jnp.repeat(mod, s8, axis=0)
    vec_blk = lambda i: jnp.where(i * tm < n_ctx, 1, 0)
    return pl.pallas_call(
        _norm_kernel,
        grid=(m // tm,),
        in_specs=[
            pl.BlockSpec((tm, d), lambda i: (i, 0)),
            pl.BlockSpec((s8, d), lambda i: (0, 0)),
            pl.BlockSpec((s8, d), lambda i: (vec_blk(i), k_shift)),
            pl.BlockSpec((s8, d), lambda i: (vec_blk(i), k_shift + 1)),
        ],
        out_specs=pl.BlockSpec((tm, d), lambda i: (i, 0)),
        out_shape=jax.ShapeDtypeStruct((m, d), BF16),
        scratch_shapes=[pltpu.VMEM((s8, d), F32)],
        compiler_params=_params(("parallel",)),
        name="norm_mod",
    )(z, jnp.broadcast_to(g.reshape(1, d), (s8, d)), mod8, mod8)


def _mm_tiles(m, n, n_ctx, wide=False):
    tm = _pick(m, (768, 512, 256, 128))
    tn = _pick(n, (1024, 512, 256, 128) if wide else (512, 256, 128))
    return tm, tn


def _resid_kernel(a_ref, w_ref, x_ref, g_ref, cs_ref, o_ref, *, tm, n_ctx):
    is_ctx = _row_ids(pl.program_id(0), tm) < n_ctx
    acc = _dot(a_ref[...], w_ref[...]) * cs_ref[...]
    o_ref[...] = x_ref[...] + _sel_rows(is_ctx, g_ref) * acc


def _mm_resid(a, w, x, mod, k_gate, n_ctx, colscale=None, groups=1, layer=None, name="mm_resid"):
    m, ka = a.shape
    n = x.shape[1]
    kg = ka // groups
    ng = n // groups
    tm, tn = _mm_tiles(m, ng, n_ctx, wide=kg <= WIDE_TILE_MAX_K)
    jn = ng // tn
    if colscale is None:
        colscale = jnp.ones((n,), F32)
    if layer is not None:
        w_spec = pl.BlockSpec((None, kg, tn), lambda i, j: (layer, 0, j))
    elif groups == 1:
        w_spec = pl.BlockSpec((kg, tn), lambda i, j: (0, j))
    else:
        w_spec = pl.BlockSpec((None, kg, tn), lambda i, j: (j // jn, 0, j % jn))
    return pl.pallas_call(
        functools.partial(_resid_kernel, tm=tm, n_ctx=n_ctx),
        grid=(m // tm, n // tn),
        in_specs=[
            pl.BlockSpec((tm, kg), lambda i, j: (i, j // jn)),
            w_spec,
            pl.BlockSpec((tm, tn), lambda i, j: (i, j)),
            pl.BlockSpec((2, tn), lambda i, j: (0, k_gate * (n // tn) + j)),
            pl.BlockSpec((1, tn), lambda i, j: (0, j)),
        ],
        out_specs=pl.BlockSpec((tm, tn), lambda i, j: (i, j)),
        out_shape=jax.ShapeDtypeStruct((m, n), F32),
        compiler_params=_params(("parallel", "parallel")),
        name=name,
    )(a, w, x, mod, colscale.reshape(1, n))


def _bias_kernel(a_ref, w_ref, b_ref, o_ref):
    o_ref[...] = _dot(a_ref[...], w_ref[...]) + b_ref[...]


def _mm_bias(a, w, b, name="mm_bias"):
    m, k = a.shape
    n = w.shape[1]
    tm, tn = _mm_tiles(m, n, 0)
    return pl.pallas_call(
        _bias_kernel,
        grid=(m // tm, n // tn),
        in_specs=[
            pl.BlockSpec((tm, k), lambda i, j: (i, 0)),
            pl.BlockSpec((k, tn), lambda i, j: (0, j)),
            pl.BlockSpec((1, tn), lambda i, j: (0, j)),
        ],
        out_specs=pl.BlockSpec((tm, tn), lambda i, j: (i, j)),
        out_shape=jax.ShapeDtypeStruct((m, n), F32),
        compiler_params=_params(("parallel", "parallel")),
        name=name,
    )(a, w, b.reshape(1, n))


def _plain_kernel(a_ref, w_ref, o_ref):
    o_ref[...] = _dot(a_ref[...], w_ref[...]).astype(o_ref.dtype)


def _mm_plain(a, w, name="mm_plain"):
    m, k = a.shape
    n = w.shape[1]
    tm, tn = _mm_tiles(m, n, 0, wide=True)
    return pl.pallas_call(
        _plain_kernel,
        grid=(m // tm, n // tn),
        in_specs=[
            pl.BlockSpec((tm, k), lambda i, j: (i, 0)),
            pl.BlockSpec((k, tn), lambda i, j: (0, j)),
        ],
        out_specs=pl.BlockSpec((tm, tn), lambda i, j: (i, j)),
        out_shape=jax.ShapeDtypeStruct((m, n), BF16),
        compiler_params=_params(("parallel", "parallel")),
        name=name,
    )(a, w)


def _zero_after(x):
    bits = pltpu.bitcast(x.astype(F32), jnp.uint32)
    return pltpu.bitcast(lax.shift_right_logical(bits, jnp.uint32(32)), F32)


def _after(value, x):
    z = jnp.sum(_zero_after(x[0:F32_SUBLANES, 0:LANES]), axis=0, keepdims=True)[:, 0:1]
    return value + z.astype(value.dtype)


def _ffn_up_kernel(a_ref, ap_ref, an_ref, wg_ref, wv_ref, cwg_ref, cwv_ref, cbg_ref, cbv_ref,
                   o_ref, ext_ref, ug_ref, uv_ref, *, tm, jn, n_steps, n_ctx, m_total):
    t = pl.program_id(0)
    h = BF16_SUBLANES
    n_ext = tm + 2 * h
    d = ext_ref.shape[1]
    kh = d // 2
    slot = t % 2

    @pl.when(t == 0)
    def _():
        ug_ref[1] = jnp.zeros(ug_ref.shape[1:], F32)
        uv_ref[1] = jnp.zeros(uv_ref.shape[1:], F32)

    @pl.when(jnp.logical_and(t % jn == 0, t < n_steps))
    def _():
        ext_ref[0:h, :] = ap_ref[...]
        ext_ref[h:h + tm, :] = a_ref[...]
        ext_ref[h + tm:h + tm + h, :] = an_ref[...]

    i_e = jnp.maximum(t - 1, 0) // jn
    rows = _row_ids(i_e, tm)
    seq_first = (rows == 0) | (rows == n_ctx)
    seq_last = (rows == n_ctx - 1) | (rows == m_total - 1)

    def conv(u, cw_ref, cb_ref):
        u_prev = pltpu.roll(u, 1, axis=0)[h:h + tm]
        u_next = pltpu.roll(u, n_ext - 1, axis=0)[h:h + tm]
        u_prev = jnp.where(seq_first, 0.0, u_prev)
        u_next = jnp.where(seq_last, 0.0, u_next)
        cw = cw_ref[...]
        return u_prev * cw[0:1] + u[h:h + tm] * cw[1:2] + u_next * cw[2:3] + cb_ref[...]

    gate = conv(ug_ref[1 - slot], cwg_ref, cbg_ref)
    val = conv(uv_ref[1 - slot], cwv_ref, cbv_ref)
    out = _silu(gate) * val
    o_ref[...] = out.astype(o_ref.dtype)

    lo = ext_ref[:, 0:kh]
    g_lo = _dot(lo, wg_ref[0:kh, :])
    v_lo = _dot(lo, wv_ref[0:kh, :])
    hi = _after(ext_ref[:, kh:d], out)
    ug_ref[slot] = g_lo + _dot(hi, wg_ref[kh:d, :])
    uv_ref[slot] = v_lo + _dot(hi, wv_ref[kh:d, :])


def _ffn_up(hn, w_up, layer, conv_w, conv_b, n_ctx):
    m, d = hn.shape
    dff = w_up.shape[2] // 2
    tm, tn = _mm_tiles(m, dff, n_ctx)
    jn = dff // tn
    n_steps = (m // tm) * jn
    h = BF16_SUBLANES
    rb = tm // h
    last_rb = m // h - 1
    i_d = lambda t: jnp.minimum(t, n_steps - 1) // jn
    j_d = lambda t: jnp.minimum(t, n_steps - 1) % jn
    i_e = lambda t: jnp.maximum(t - 1, 0) // jn
    j_e = lambda t: jnp.maximum(t - 1, 0) % jn
    return pl.pallas_call(
        functools.partial(_ffn_up_kernel, tm=tm, jn=jn, n_steps=n_steps, n_ctx=n_ctx, m_total=m),
        grid=(n_steps + 1,),
        in_specs=[
            pl.BlockSpec((tm, d), lambda t: (i_d(t), 0)),
            pl.BlockSpec((h, d), lambda t: (jnp.maximum(i_d(t) * rb - 1, 0), 0)),
            pl.BlockSpec((h, d), lambda t: (jnp.minimum((i_d(t) + 1) * rb, last_rb), 0)),
            pl.BlockSpec((None, d, tn), lambda t: (layer, 0, j_d(t))),
            pl.BlockSpec((None, d, tn), lambda t: (layer, 0, jn + j_d(t))),
            pl.BlockSpec((CONV_WIDTH, tn), lambda t: (0, j_e(t))),
            pl.BlockSpec((CONV_WIDTH, tn), lambda t: (0, jn + j_e(t))),
            pl.BlockSpec((1, tn), lambda t: (0, j_e(t))),
            pl.BlockSpec((1, tn), lambda t: (0, jn + j_e(t))),
        ],
        out_specs=pl.BlockSpec((tm, tn), lambda t: (i_e(t), j_e(t))),
        out_shape=jax.ShapeDtypeStruct((m, dff), BF16),
        scratch_shapes=[
            pltpu.VMEM((tm + 2 * h, d), BF16),
            pltpu.VMEM((2, tm + 2 * h, tn), F32),
            pltpu.VMEM((2, tm + 2 * h, tn), F32),
        ],
        compiler_params=_params(("arbitrary",)),
        name="ffn_up",
    )(hn, hn, hn, w_up, w_up, conv_w, conv_w, conv_b.reshape(1, -1), conv_b.reshape(1, -1))


def _conv_ffn(z, norm_g, mod, w_up, w_down, layer, conv_w, conv_b, n_ctx):
    hn = _norm_mod(z, norm_g, mod, 3, n_ctx)
    act = _ffn_up(hn, w_up, layer, conv_w, conv_b, n_ctx)
    return _mm_resid(act, w_down, z, mod, 5, n_ctx, layer=layer, name="ffn_down")


def _pool_kernel(x_ref, xp_ref, xn_ref, g_ref, sh_ref, sc_ref, o_ref, s_ref, a_ref, *, tm, n_ctx, m_total):
    i = pl.program_id(0)
    p = POOL_HALO
    d = x_ref.shape[1]
    cg = d // len(POOL_WINDOWS)
    a_ref[...] = g_ref[...] * (1.0 + sc_ref[...])

    def modulated(x):
        ms = jnp.mean(x * x, axis=-1, keepdims=True)
        return x * lax.rsqrt(ms + EPS) * a_ref[...] + sh_ref[...]

    start = i * tm
    stop = start + tm
    prev_ok = jnp.logical_and(start != 0, start != n_ctx)
    next_ok = jnp.logical_and(stop != n_ctx, stop != m_total)
    s_ref[0:p, :] = jnp.where(prev_ok, modulated(xp_ref[...]), 0.0)
    s_ref[p + tm:p + tm + p, :] = jnp.where(next_ok, modulated(xn_ref[...]), 0.0)

    def body(r, carry):
        base = pl.multiple_of(r * p, p)
        s_ref[pl.ds(p + base, p), :] = modulated(x_ref[pl.ds(base, p), :])
        return carry

    lax.fori_loop(0, tm // p, body, 0, unroll=NORM_UNROLL)

    in_ctx = start < n_ctx
    t0 = jnp.where(in_ctx, start, start - n_ctx)
    t_len = jnp.where(in_ctx, n_ctx, m_total - n_ctx)
    nr = POOL_ROWS
    for rb in range(0, tm, nr):
        t = t0 + rb + lax.broadcasted_iota(jnp.int32, (nr, 1), 0)
        for gi, w in enumerate(POOL_WINDOWS):
            cols = slice(gi * cg, (gi + 1) * cg)
            acc = s_ref[p + rb - w // 2:p + rb - w // 2 + nr, cols]
            for k in range(1 - w // 2, w - w // 2):
                acc = acc + s_ref[p + rb + k:p + rb + k + nr, cols]
            lo = jnp.maximum(t - w // 2, 0)
            hi = jnp.minimum(t + w - w // 2, t_len)
            cnt = (hi - lo).astype(F32)
            o_ref[rb:rb + nr, cols] = (acc / cnt - s_ref[p + rb:p + rb + nr, cols]).astype(o_ref.dtype)


def _pool_pre(z, g, mod, n_ctx):
    m, d = z.shape
    tm = _pick(n_ctx, (256, 128, 64, 32))
    p = POOL_HALO
    rb = tm // p
    last_rb = m // p - 1
    mod8 = jnp.repeat(mod, p, axis=0)
    vec_blk = lambda i: jnp.where(i * tm < n_ctx, 1, 0)
    return pl.pallas_call(
        functools.partial(_pool_kernel, tm=tm, n_ctx=n_ctx, m_total=m),
        grid=(m // tm,),
        in_specs=[
            pl.BlockSpec((tm, d), lambda i: (i, 0)),
            pl.BlockSpec((p, d), lambda i: (jnp.maximum(i * rb - 1, 0), 0)),
            pl.BlockSpec((p, d), lambda i: (jnp.minimum((i + 1) * rb, last_rb), 0)),
            pl.BlockSpec((p, d), lambda i: (0, 0)),
            pl.BlockSpec((p, d), lambda i: (vec_blk(i), 0)),
            pl.BlockSpec((p, d), lambda i: (vec_blk(i), 1)),
        ],
        out_specs=pl.BlockSpec((tm, d), lambda i: (i, 0)),
        out_shape=jax.ShapeDtypeStruct((m, d), BF16),
        scratch_shapes=[pltpu.VMEM((tm + 2 * p, d), F32), pltpu.VMEM((p, d), F32)],
        compiler_params=_params(("parallel",)),
        name="pool_pre",
    )(z, z, z, jnp.broadcast_to(g.reshape(1, d), (p, d)), mod8, mod8)


def _pair_cast_kernel(w_ref, o_ref, *, hd, scale):
    r = lax.broadcasted_iota(jnp.int32, (hd, hd), 0)
    c = lax.broadcasted_iota(jnp.int32, (hd, hd), 1)
    src_col = jnp.where(c < hd // 2, 2 * c, 2 * (c - hd // 2) + 1)
    perm = jnp.where(r == src_col, 1.0, 0.0).astype(BF16)
    for hh in range(w_ref.shape[1] // hd):
        cols = slice(hh * hd, (hh + 1) * hd)
        wh = (w_ref[:, cols] * scale).astype(BF16)
        o_ref[:, cols] = _dot(wh, perm).astype(BF16)


def _pair_cast(w, hd, scale=1.0):
    k, n = w.shape
    tk = _pick(k, (1024, 512, 256, 128))
    tn = _pick(n, (1024, 512, 256, 128))
    return pl.pallas_call(
        functools.partial(_pair_cast_kernel, hd=hd, scale=scale),
        grid=(k // tk, n // tn),
        in_specs=[pl.BlockSpec((tk, tn), lambda i, j: (i, j))],
        out_specs=pl.BlockSpec((tk, tn), lambda i, j: (i, j)),
        out_shape=jax.ShapeDtypeStruct((k, n), BF16),
        compiler_params=_params(("parallel", "parallel")),
        name="pair_cast",
    )(w)


def _head_proj_kernel(a_ref, w_ref, g_ref, c_ref, s_ref, o_ref, *, hd, normalize):
    acc = _dot(a_ref[...], w_ref[...])
    cos = c_ref[...]
    sin = s_ref[...]
    mean_w = jnp.full((2 * hd, hd), 1.0 / hd, BF16)
    for hh in range(acc.shape[1] // hd):
        cols = slice(hh * hd, (hh + 1) * hd)
        y = acc[:, cols]
        if normalize:
            sq = y * y
            hi = sq.astype(BF16)
            lo = (sq - hi.astype(F32)).astype(BF16)
            ms = _dot(jnp.concatenate([hi, lo], axis=1), mean_w)
            y = y * lax.rsqrt(ms + EPS) * g_ref[:, cols]
        o_ref[:, cols] = (y * cos + pltpu.roll(y, hd // 2, axis=1) * sin).astype(o_ref.dtype)


def _head_proj(hn, w, gain, cos, sin, hd, normalize, name):
    m, d = hn.shape
    n = w.shape[1]
    tm, tn = _mm_tiles(m, n, 0, wide=True)
    tn = max(tn, hd)
    return pl.pallas_call(
        functools.partial(_head_proj_kernel, hd=hd, normalize=normalize),
        grid=(m // tm, n // tn),
        in_specs=[
            pl.BlockSpec((tm, d), lambda i, j: (i, 0)),
            pl.BlockSpec((d, tn), lambda i, j: (0, j)),
            pl.BlockSpec((1, tn), lambda i, j: (0, j)),
            pl.BlockSpec((tm, hd), lambda i, j: (i, 0)),
            pl.BlockSpec((tm, hd), lambda i, j: (i, 0)),
        ],
        out_specs=pl.BlockSpec((tm, tn), lambda i, j: (i, j)),
        out_shape=jax.ShapeDtypeStruct((m, n), BF16),
        compiler_params=_params(("parallel", "parallel")),
        name=name,
    )(hn, w, gain.reshape(1, n), cos, sin)


def _flash_kernel(q_ref, kt_ref, v_ref, o_ref, qs_ref, vx_ref, *, tq, chunk):
    hd = ATTN_HEAD_DIM
    n_keys = v_ref.shape[0]

    @pl.when(pl.program_id(1) == 0)
    def _():
        vx_ref[:, 0:hd] = v_ref[...]
        lane = lax.broadcasted_iota(jnp.int32, (n_keys, hd), 1)
        vx_ref[:, hd:2 * hd] = jnp.where(lane == 0, 1.0, 0.0).astype(BF16)

    for gi in range(ATTN_GROUP):
        qs_ref[gi * tq:(gi + 1) * tq, :] = q_ref[:, gi * hd:(gi + 1) * hd]
    q = qs_ref[...]
    m_run = None
    acc = None
    for start in range(0, n_keys, chunk):
        size = min(chunk, n_keys - start)
        s = _dot(q, kt_ref[:, start:start + size])
        m_c = jnp.max(s, axis=-1, keepdims=True)
        m_new = m_c if m_run is None else jnp.maximum(m_run, m_c)
        p = jnp.exp2(s - m_new).astype(BF16)
        pv = _dot(p, vx_ref[start:start + size, :])
        acc = pv if acc is None else jnp.exp2(m_run - m_new) * acc + pv
        m_run = m_new
    out = acc[:, 0:hd] / acc[:, hd:hd + 1]
    for gi in range(ATTN_GROUP):
        o_ref[:, gi * hd:(gi + 1) * hd] = out[gi * tq:(gi + 1) * tq].astype(o_ref.dtype)


def _flash_part(q, kt, v, row0, n_rows, n_keys):
    hd = ATTN_HEAD_DIM
    kvh = v.shape[1] // hd
    tq = _pick(n_rows, (256, 128, 64, 32, 16))
    chunk = min(n_keys, FLASH_KEY_CHUNK)
    gw = ATTN_GROUP * hd
    r0 = row0 // tq
    return pl.pallas_call(
        functools.partial(_flash_kernel, tq=tq, chunk=chunk),
        grid=(kvh, n_rows // tq),
        in_specs=[
            pl.BlockSpec((tq, gw), lambda g, i: (r0 + i, g)),
            pl.BlockSpec((hd, n_keys), lambda g, i: (g, 0)),
            pl.BlockSpec((n_keys, hd), lambda g, i: (0, g)),
        ],
        out_specs=pl.BlockSpec((tq, gw), lambda g, i: (i, g)),
        out_shape=jax.ShapeDtypeStruct((n_rows, q.shape[1]), BF16),
        scratch_shapes=[
            pltpu.VMEM((ATTN_GROUP * tq, hd), BF16),
            pltpu.VMEM((n_keys, 2 * hd), BF16),
        ],
        compiler_params=_params(("parallel", "arbitrary")),
        name="flash_attn",
    )(q, kt, v)


def _flash(q, k, v, n_ctx):
    m = q.shape[0]
    kt = k.T
    o_lat = _flash_part(q, kt, v, n_ctx, m - n_ctx, m)
    o_ctx = _flash_part(q, kt, v, 0, n_ctx, n_ctx)
    return jnp.concatenate([o_ctx, o_lat], axis=0)


def _gate_prep_kernel(x_ref, o_ref, *, n_heads):
    x = x_ref[...]
    rows = x.shape[0]
    col = lax.broadcasted_iota(jnp.int32, x.shape, 1)
    is_f = jnp.logical_and((col // n_heads) % 2 == 1, col < 4 * n_heads)
    backward = col >= 2 * n_heads
    logf = _log_sigmoid(x)
    r = lax.broadcasted_iota(jnp.int32, (rows, rows), 0)
    c = lax.broadcasted_iota(jnp.int32, (rows, rows), 1)
    tril = (c <= r).astype(F32)
    triu = (c >= r).astype(F32)
    fwd = jnp.dot(tril, logf, preferred_element_type=F32, precision=lax.Precision.HIGHEST)
    bwd = jnp.dot(triu, logf, preferred_element_type=F32, precision=lax.Precision.HIGHEST)
    o_ref[...] = jnp.where(is_f, jnp.where(backward, bwd, fwd), x)


def _gate_prep(gates, n_heads):
    m, n = gates.shape
    ch = SCAN_CHUNK
    return pl.pallas_call(
        functools.partial(_gate_prep_kernel, n_heads=n_heads),
        grid=(m // ch,),
        in_specs=[pl.BlockSpec((ch, n), lambda c: (c, 0))],
        out_specs=pl.BlockSpec((ch, n), lambda c: (c, 0)),
        out_shape=jax.ShapeDtypeStruct((m, n), F32),
        compiler_params=_params(("parallel",)),
        name="gate_prep",
    )(gates)


def _scan_block(n_blk):
    def blk(d, c):
        return jnp.where(c == 0, 0, jnp.where(d == 0, c, n_blk - c))
    return blk


def _mlstm_kernel(q_ref, k_ref, v_ref, gc_ref, ir_ref, br_ref, o_ref, s_ref, n_ref, m_ref, *, n_heads, hp):
    d = pl.program_id(0)
    ch = q_ref.shape[0]
    dk = q_ref.shape[1] // hp
    dv = v_ref.shape[1] // hp

    @pl.when(pl.program_id(2) == 0)
    def _():
        s_ref[...] = jnp.zeros(s_ref.shape, F32)
        n_ref[...] = jnp.zeros(n_ref.shape, F32)
        m_ref[...] = jnp.zeros(m_ref.shape, F32)

    gc = gc_ref[...]
    lane = lax.broadcasted_iota(jnp.int32, gc.shape, 1)
    r = lax.broadcasted_iota(jnp.int32, (ch, ch), 0)
    c = lax.broadcasted_iota(jnp.int32, (ch, ch), 1)
    sgn = 1 - 2 * d
    seen = (r - c) * sgn >= 0

    for hl in range(hp):
        q = q_ref[:, hl * dk:(hl + 1) * dk]
        k = k_ref[:, hl * dk:(hl + 1) * dk]
        v = v_ref[:, hl * dv:(hl + 1) * dv]
        base = d * 2 * n_heads + pl.program_id(1) * hp + hl
        i_col = jnp.sum(jnp.where(lane == base, gc, 0.0), axis=1, keepdims=True)
        b_col = jnp.sum(jnp.where(lane == base + n_heads, gc, 0.0), axis=1, keepdims=True)
        i_row = ir_ref[hl]
        b_row = br_ref[hl]
        m_prev = m_ref[hl]
        s_prev = s_ref[hl]
        n_prev = n_ref[hl]

        log_d = jnp.where(seen, b_col - b_row + i_row, -jnp.inf)
        log_prev = b_col + m_prev
        m_t = jnp.maximum(log_prev, jnp.max(log_d, axis=-1, keepdims=True))
        qk = lax.dot_general(q, k, (((1,), (1,)), ((), ())), preferred_element_type=F32)
        s = qk * jnp.exp(log_d - m_t)
        w_prev = jnp.exp(log_prev - m_t)
        num = _dot(s.astype(BF16), v) + w_prev * _dot(q, s_prev.astype(BF16))
        den = (jnp.sum(s, axis=-1, keepdims=True)
               + w_prev * jnp.sum(q.astype(F32) * n_prev, axis=-1, keepdims=True))
        o_ref[:, hl * dv:(hl + 1) * dv] = num / jnp.maximum(jnp.abs(den), jnp.exp(-m_t))

        b_last = jnp.where(d == 0, b_row[:, ch - 1:ch], b_row[:, 0:1])
        log_w_row = b_last - b_row + i_row
        m_new = jnp.maximum(b_last + m_prev, jnp.max(log_w_row, axis=-1, keepdims=True))
        w_col = jnp.exp(b_last - b_col + i_col - m_new)
        decay = jnp.exp(b_last + m_prev - m_new)
        wv = (w_col * v.astype(F32)).astype(BF16)
        s_ref[hl] = decay * s_prev + lax.dot_general(k, wv, (((0,), (0,)), ((), ())),
                                                     preferred_element_type=F32)
        n_ref[hl] = decay * n_prev + jnp.sum(w_col * k.astype(F32), axis=0, keepdims=True)
        m_ref[hl] = m_new


def _mlstm_scan(q, k, v, g2, g2t, n_heads, dk, dv):
    m = q.shape[0]
    ch = SCAN_CHUNK
    n_blk = m // ch
    blk = _scan_block(n_blk)
    hp = SCAN_HEADS_PER_STEP
    hb = n_heads // hp
    return pl.pallas_call(
        functools.partial(_mlstm_kernel, n_heads=n_heads, hp=hp),
        grid=(2, hb, n_blk),
        in_specs=[
            pl.BlockSpec((ch, hp * dk), lambda d, h, c: (blk(d, c), h)),
            pl.BlockSpec((ch, hp * dk), lambda d, h, c: (blk(d, c), h)),
            pl.BlockSpec((ch, hp * dv), lambda d, h, c: (blk(d, c), h)),
            pl.BlockSpec((ch, g2.shape[1]), lambda d, h, c: (blk(d, c), 0)),
            pl.BlockSpec((hp, 1, ch), lambda d, h, c: (d * 2 * hb + h, 0, blk(d, c))),
            pl.BlockSpec((hp, 1, ch), lambda d, h, c: (d * 2 * hb + hb + h, 0, blk(d, c))),
        ],
        out_specs=pl.BlockSpec((None, ch, hp * dv), lambda d, h, c: (d, blk(d, c), h)),
        out_shape=jax.ShapeDtypeStruct((2, m, n_heads * dv), F32),
        scratch_shapes=[
            pltpu.VMEM((hp, dk, dv), F32),
            pltpu.VMEM((hp, 1, dk), F32),
            pltpu.VMEM((hp, 1, 1), F32),
        ],
        compiler_params=_params(("parallel", "parallel", "arbitrary")),
        name="mlstm_scan",
    )(q, k, v, g2, g2t, g2t)


def _gated_norm_kernel(a_ref, w_ref, hf_ref, hb_ref, g_ref, o_ref, *, hw, act):
    acc = _dot(a_ref[...], w_ref[...])
    for hh in range(acc.shape[1] // hw):
        cols = slice(hh * hw, (hh + 1) * hw)
        y = hf_ref[:, cols] + hb_ref[:, cols]
        ms = jnp.mean(y * y, axis=-1, keepdims=True)
        y = y * lax.rsqrt(ms + EPS) * g_ref[:, cols]
        o_ref[:, cols] = (act(acc[:, cols]) * y).astype(o_ref.dtype)


def _mm_gated_norm(hn, w, hfb, gain, hw, act, name):
    m, d = hn.shape
    n = w.shape[1]
    tm, tn = _mm_tiles(m, n, 0, wide=True)
    tn = max(tn, hw)
    return pl.pallas_call(
        functools.partial(_gated_norm_kernel, hw=hw, act=act),
        grid=(m // tm, n // tn),
        in_specs=[
            pl.BlockSpec((tm, d), lambda i, j: (i, 0)),
            pl.BlockSpec((d, tn), lambda i, j: (0, j)),
            pl.BlockSpec((None, tm, tn), lambda i, j: (0, i, j)),
            pl.BlockSpec((None, tm, tn), lambda i, j: (1, i, j)),
            pl.BlockSpec((1, tn), lambda i, j: (0, j)),
        ],
        out_specs=pl.BlockSpec((tm, tn), lambda i, j: (i, j)),
        out_shape=jax.ShapeDtypeStruct((m, n), BF16),
        compiler_params=_params(("parallel", "parallel")),
        name=name,
    )(hn, w, hfb, hfb, gain.reshape(1, n))


def _ret_kernel(q_ref, k_ref, v_ref, lg_ref, o_ref, s_ref, dec_ref, xi_ref, zeta_ref, *, hp):
    d = pl.program_id(0)
    ch = q_ref.shape[0]
    hd = q_ref.shape[1] // hp

    @pl.when(pl.program_id(2) == 0)
    def _():
        s_ref[...] = jnp.zeros(s_ref.shape, F32)
        r = lax.broadcasted_iota(jnp.int32, (ch, ch), 0)
        c = lax.broadcasted_iota(jnp.int32, (ch, ch), 1)
        sgn = 1 - 2 * d
        rel = (r - c) * sgn
        seen = rel >= 0
        t = lax.broadcasted_iota(jnp.int32, (ch, hd), 0)
        pos = jnp.where(d == 0, t, ch - 1 - t).astype(F32)
        for hl in range(hp):
            lg = _log_sigmoid(lg_ref[hl])[:, 0:1]
            dec_ref[hl] = jnp.where(seen, jnp.exp(lg * jnp.where(seen, rel, 0).astype(F32)), 0.0)
            xi_ref[hl] = jnp.exp(lg * (pos + 1.0))
            zeta_ref[hl] = jnp.exp(lg * (ch - 1.0 - pos))

    for hl in range(hp):
        q = q_ref[:, hl * hd:(hl + 1) * hd]
        k = k_ref[:, hl * hd:(hl + 1) * hd]
        v = v_ref[:, hl * hd:(hl + 1) * hd]
        s_prev = s_ref[hl]
        gamma_chunk = jnp.exp(_log_sigmoid(lg_ref[hl])[:, 0:1] * ch)
        qk = lax.dot_general(q, k, (((1,), (1,)), ((), ())), preferred_element_type=F32)
        sc = (qk * dec_ref[hl]).astype(BF16)
        o_ref[:, hl * hd:(hl + 1) * hd] = _dot(sc, v) + _dot(q, s_prev.astype(BF16)) * xi_ref[hl]
        kz = (k.astype(F32) * zeta_ref[hl]).astype(BF16)
        s_ref[hl] = gamma_chunk * s_prev + lax.dot_general(kz, v, (((0,), (0,)), ((), ())),
                                                           preferred_element_type=F32)


def _ret_scan(q, k, v, decay_logit, n_heads, hd):
    m = q.shape[0]
    ch = SCAN_CHUNK
    n_blk = m // ch
    blk = _scan_block(n_blk)
    hp = SCAN_HEADS_PER_STEP
    hb = n_heads // hp
    lg = jnp.broadcast_to(decay_logit.astype(F32).reshape(2 * n_heads, 1, 1), (2 * n_heads, 1, LANES))
    return pl.pallas_call(
        functools.partial(_ret_kernel, hp=hp),
        grid=(2, hb, n_blk),
        in_specs=[
            pl.BlockSpec((ch, hp * hd), lambda d, h, c: (blk(d, c), h)),
            pl.BlockSpec((ch, hp * hd), lambda d, h, c: (blk(d, c), h)),
            pl.BlockSpec((ch, hp * hd), lambda d, h, c: (blk(d, c), h)),
            pl.BlockSpec((hp, 1, LANES), lambda d, h, c: (d * hb + h, 0, 0)),
        ],
        out_specs=pl.BlockSpec((None, ch, hp * hd), lambda d, h, c: (d, blk(d, c), h)),
        out_shape=jax.ShapeDtypeStruct((2, m, n_heads * hd), F32),
        scratch_shapes=[
            pltpu.VMEM((hp, hd, hd), F32),
            pltpu.VMEM((hp, ch, ch), F32),
            pltpu.VMEM((hp, ch, hd), F32),
            pltpu.VMEM((hp, ch, hd), F32),
        ],
        compiler_params=_params(("parallel", "parallel", "arbitrary")),
        name="ret_scan",
    )(q, k, v, lg)


def _split_pairs(w, n_heads, hd):
    lead = w.shape[:-1]
    return w.reshape(*lead, n_heads, hd // 2, 2).swapaxes(-1, -2).reshape(*lead, n_heads * hd)


def _rope_tables(n_lat, n_ctx, head_dim):
    t = jnp.arange(n_lat)
    row = (t // GRID_W).astype(F32)
    col = (t % GRID_W).astype(F32)
    axis_dim = head_dim // 2
    inv_freq = 1.0 / (ROPE_THETA ** (jnp.arange(0, axis_dim, 2, dtype=F32) / axis_dim))
    ang = jnp.concatenate([row[:, None] * inv_freq, col[:, None] * inv_freq], axis=-1)
    cos = jnp.concatenate([jnp.ones((n_ctx, head_dim // 2), F32), jnp.cos(ang)], axis=0)
    sin = jnp.concatenate([jnp.zeros((n_ctx, head_dim // 2), F32), jnp.sin(ang)], axis=0)
    return jnp.concatenate([cos, cos], axis=1), jnp.concatenate([-sin, sin], axis=1)


def kernel(x, c, ctx, c_ctx, ada_w, ada_b, norm1_g, norm2_g, ffn_w_up, ffn_conv_w, ffn_conv_b, ffn_w_down, pool_w, pool_scale, attn_wq, attn_wk, attn_wv, attn_wo, attn_q_norm, attn_k_norm, mlstm_wq, mlstm_wk, mlstm_wv, mlstm_w_gates, mlstm_b_gates, mlstm_w_ogate, mlstm_out_norm, mlstm_wo, ret_wq, ret_wk, ret_wv, ret_wg, ret_decay_logit, ret_out_norm, ret_wo):
    batch, n_lat, d = x.shape
    n_ctx = ctx.shape[1]
    depth = ada_w.shape[0]
    assert batch == 1 and n_ctx % BF16_SUBLANES == 0 and n_lat % n_ctx == 0

    z = jnp.concatenate([ctx[0], x[0]], axis=0)
    cc = jnp.zeros((8, d), F32).at[0].set(c[0]).at[1].set(c_ctx)
    mods = _ada_mods(cc, ada_w, ada_b)
    w_up = ffn_w_up.astype(BF16)
    w_down = ffn_w_down.astype(BF16)

    for i in range(depth):
        kind, j = i % 4, i // 4
        mod = mods[i, :2]
        if kind == 0:
            pooled = _pool_pre(z, norm1_g[i], mod, n_ctx)
            z = _mm_resid(pooled, pool_w[j].astype(BF16), z, mod, 2, n_ctx, colscale=pool_scale[j],
                          groups=len(POOL_WINDOWS), name="pool_mix")
        elif kind == 1:
            hd = ATTN_HEAD_DIM
            qh = attn_wq.shape[2] // hd
            kvh = attn_wk.shape[2] // hd
            hn = _norm_mod(z, norm1_g[i], mod, 0, n_ctx)
            cos, sin = _rope_tables(n_lat, n_ctx, hd)
            gq = jnp.tile(_split_pairs(attn_q_norm[j], 1, hd) * (hd ** -0.5 * LOG2_E), qh)
            gk = jnp.tile(_split_pairs(attn_k_norm[j], 1, hd), kvh)
            q = _head_proj(hn, _pair_cast(attn_wq[j], hd), gq, cos, sin, hd, True, "attn_q")
            k = _head_proj(hn, _pair_cast(attn_wk[j], hd), gk, cos, sin, hd, True, "attn_k")
            v = _mm_plain(hn, attn_wv[j].astype(BF16), name="attn_v")
            o = _flash(q, k, v, n_ctx)
            z = _mm_resid(o, attn_wo[j].astype(BF16), z, mod, 2, n_ctx, name="attn_out")
        elif kind == 2:
            nh = MLSTM_HEADS
            dk = mlstm_wq.shape[2] // nh
            dv = mlstm_wv.shape[2] // nh
            hn = _norm_mod(z, norm1_g[i], mod, 0, n_ctx)
            q = _mm_plain(hn, mlstm_wq[j].astype(BF16), name="mlstm_q")
            k = _mm_plain(hn, (mlstm_wk[j] * dk ** -0.5).astype(BF16), name="mlstm_k")
            v = _mm_plain(hn, mlstm_wv[j].astype(BF16), name="mlstm_v")
            wg = jnp.zeros((d, LANES), F32).at[:, :4 * nh].set(mlstm_w_gates[j]).astype(BF16)
            bg = jnp.zeros((LANES,), F32).at[:4 * nh].set(mlstm_b_gates[j])
            g2 = _gate_prep(_mm_bias(hn, wg, bg, name="mlstm_gates"), nh)
            g2t = g2.T.reshape(LANES, 1, -1)
            hfb = _mlstm_scan(q, k, v, g2, g2t, nh, dk, dv)
            gated = _mm_gated_norm(hn, mlstm_w_ogate[j].astype(BF16), hfb, mlstm_out_norm[j], dv,
                                   _sigmoid, "mlstm_ogate")
            z = _mm_resid(gated, mlstm_wo[j].astype(BF16), z, mod, 2, n_ctx, name="mlstm_out")
        else:
            nh = RET_HEADS
            hd = ret_wq.shape[2] // nh
            hn = _norm_mod(z, norm1_g[i], mod, 0, n_ctx)
            cos, sin = _rope_tables(n_lat, n_ctx, hd)
            ones = jnp.ones((nh * hd,), F32)
            q = _head_proj(hn, _pair_cast(ret_wq[j], hd), ones, cos, sin, hd, False, "ret_q")
            k = _head_proj(hn, _pair_cast(ret_wk[j], hd, hd ** -0.5), ones, cos, sin, hd, False, "ret_k")
            v = _mm_plain(hn, ret_wv[j].astype(BF16), name="ret_v")
            ofb = _ret_scan(q, k, v, ret_decay_logit[j], nh, hd)
            gated = _mm_gated_norm(hn, ret_wg[j].astype(BF16), ofb, ret_out_norm[j], hd, _silu, "ret_gate")
            z = _mm_resid(gated, ret_wo[j].astype(BF16), z, mod, 2, n_ctx, name="ret_out")
        z = _conv_ffn(z, norm2_g[i], mod, w_up, w_down, i, ffn_conv_w[i], ffn_conv_b[i], n_ctx)
    return z[n_ctx:][None]
```

```python
import functools

import jax
import jax.numpy as jnp
from jax import lax
from jax.experimental import pallas as pl
from jax.experimental.pallas import tpu as pltpu

F32 = jnp.float32
BF16 = jnp.bfloat16

EPS = 1e-6
GRID_W = 64
ROPE_THETA = 10000.0
N_MOD = 6
POOL_WINDOWS = (2, 4, 8, 16)
ATTN_HEAD_DIM = 128
ATTN_GROUP = 4
MLSTM_HEADS = 8
RET_HEADS = 16
CONV_WIDTH = 3
LOG2_E = 1.4426950408889634

VMEM_LIMIT_BYTES = 56 * 1024 * 1024
BF16_SUBLANES = 16
F32_SUBLANES = 8
LANES = 128
WIDE_TILE_MAX_K = 4096
FLASH_KEY_CHUNK = 1280
SCAN_CHUNK = 256
SCAN_HEADS_PER_STEP = 4
RET_HEADS_PER_STEP = 16
NORM_UNROLL = 4
POOL_HALO = 8
POOL_ROWS = 32


def _pick(n, candidates):
    for c in candidates:
        if c <= n and n % c == 0:
            return c
    return n


def _params(sem):
    return pltpu.CompilerParams(dimension_semantics=sem, vmem_limit_bytes=VMEM_LIMIT_BYTES)


def _silu(x):
    return x * (1.0 / (1.0 + jnp.exp(-x)))


def _sigmoid(x):
    return 1.0 / (1.0 + jnp.exp(-x))


def _log_sigmoid(x):
    return jnp.minimum(x, 0.0) - jnp.log(1.0 + jnp.exp(-jnp.abs(x)))


def _row_ids(i, tm):
    return i * tm + lax.broadcasted_iota(jnp.int32, (tm, 1), 0)


def _sel_rows(is_ctx, ref2):
    return jnp.where(is_ctx, ref2[1:2, :], ref2[0:1, :])


def _dot(a, b):
    return jnp.dot(a, b, preferred_element_type=F32)


def _ada_kernel(a_ref, w_ref, b_ref, o_ref):
    a = _silu(a_ref[...]).astype(BF16)
    o_ref[...] = _dot(a, w_ref[...].astype(BF16)) + b_ref[...]


def _ada_mods(cc, ada_w, ada_b):
    depth, d, n = ada_w.shape
    tn = _pick(n, (512, 256, 128))
    rows = cc.shape[0]
    return pl.pallas_call(
        _ada_kernel,
        grid=(depth, n // tn),
        in_specs=[
            pl.BlockSpec((rows, d), lambda l, j: (0, 0)),
            pl.BlockSpec((None, d, tn), lambda l, j: (l, 0, j)),
            pl.BlockSpec((None, 1, tn), lambda l, j: (l, 0, j)),
        ],
        out_specs=pl.BlockSpec((None, rows, tn), lambda l, j: (l, 0, j)),
        out_shape=jax.ShapeDtypeStruct((depth, rows, n), F32),
        compiler_params=_params(("parallel", "parallel")),
        name="ada_mods",
    )(cc, ada_w, ada_b.reshape(depth, 1, n))


def _norm_kernel(x_ref, g_ref, sh_ref, sc_ref, o_ref, a_ref):
    a_ref[...] = g_ref[...] * (1.0 + sc_ref[...])
    s8 = F32_SUBLANES

    def body(r, carry):
        base = pl.multiple_of(r * BF16_SUBLANES, BF16_SUBLANES)
        ys = []
        for part in range(BF16_SUBLANES // s8):
            x = x_ref[pl.ds(pl.multiple_of(base + part * s8, s8), s8), :]
            ms = jnp.mean(x * x, axis=-1, keepdims=True)
            ys.append(x * lax.rsqrt(ms + EPS) * a_ref[...] + sh_ref[...])
        o_ref[pl.ds(base, BF16_SUBLANES), :] = jnp.concatenate(ys, axis=0).astype(o_ref.dtype)
        return carry

    lax.fori_loop(0, x_ref.shape[0] // BF16_SUBLANES, body, 0, unroll=NORM_UNROLL)


def _norm_mod(z, g, mod, k_shift, n_ctx):
    m, d = z.shape
    s8 = F32_SUBLANES
    tm = _pick(n_ctx, (256, 128, 64, 32, 16))
    mod8 = jnp.repeat(mod, s8, axis=0)
    vec_blk = lambda i: jnp.where(i * tm < n_ctx, 1, 0)
    return pl.pallas_call(
        _norm_kernel,
        grid=(m // tm,),
        in_specs=[
            pl.BlockSpec((tm, d), lambda i: (i, 0)),
            pl.BlockSpec((s8, d), lambda i: (0, 0)),
            pl.BlockSpec((s8, d), lambda i: (vec_blk(i), k_shift)),
            pl.BlockSpec((s8, d), lambda i: (vec_blk(i), k_shift + 1)),
        ],
        out_specs=pl.BlockSpec((tm, d), lambda i: (i, 0)),
        out_shape=jax.ShapeDtypeStruct((m, d), BF16),
        scratch_shapes=[pltpu.VMEM((s8, d), F32)],
        compiler_params=_params(("parallel",)),
        name="norm_mod",
    )(z, jnp.broadcast_to(g.reshape(1, d), (s8, d)), mod8, mod8)


def _mm_tiles(m, n, n_ctx, wide=False):
    tm = _pick(m, (768, 512, 256, 128))
    tn = _pick(n, (1024, 512, 256, 128) if wide else (512, 256, 128))
    return tm, tn


def _resid_kernel(a_ref, w_ref, x_ref, g_ref, cs_ref, o_ref, *, tm, n_ctx):
    is_ctx = _row_ids(pl.program_id(0), tm) < n_ctx
    acc = _dot(a_ref[...], w_ref[...]) * cs_ref[...]
    o_ref[...] = x_ref[...] + _sel_rows(is_ctx, g_ref) * acc


def _mm_resid(a, w, x, mod, k_gate, n_ctx, colscale=None, groups=1, layer=None, name="mm_resid"):
    m, ka = a.shape
    n = x.shape[1]
    kg = ka // groups
    ng = n // groups
    tm, tn = _mm_tiles(m, ng, n_ctx, wide=kg <= WIDE_TILE_MAX_K)
    jn = ng // tn
    if colscale is None:
        colscale = jnp.ones((n,), F32)
    if layer is not None:
        w_spec = pl.BlockSpec((None, kg, tn), lambda i, j: (layer, 0, j))
    elif groups == 1:
        w_spec = pl.BlockSpec((kg, tn), lambda i, j: (0, j))
    else:
        w_spec = pl.BlockSpec((None, kg, tn), lambda i, j: (j // jn, 0, j % jn))
    return pl.pallas_call(
        functools.partial(_resid_kernel, tm=tm, n_ctx=n_ctx),
        grid=(m // tm, n // tn),
        in_specs=[
            pl.BlockSpec((tm, kg), lambda i, j: (i, j // jn)),
            w_spec,
            pl.BlockSpec((tm, tn), lambda i, j: (i, j)),
            pl.BlockSpec((2, tn), lambda i, j: (0, k_gate * (n // tn) + j)),
            pl.BlockSpec((1, tn), lambda i, j: (0, j)),
        ],
        out_specs=pl.BlockSpec((tm, tn), lambda i, j: (i, j)),
        out_shape=jax.ShapeDtypeStruct((m, n), F32),
        compiler_params=_params(("parallel", "parallel")),
        name=name,
    )(a, w, x, mod, colscale.reshape(1, n))


def _bias_kernel(a_ref, w_ref, b_ref, o_ref):
    o_ref[...] = _dot(a_ref[...], w_ref[...]) + b_ref[...]


def _mm_bias(a, w, b, name="mm_bias"):
    m, k = a.shape
    n = w.shape[1]
    tm, tn = _mm_tiles(m, n, 0)
    return pl.pallas_call(
        _bias_kernel,
        grid=(m // tm, n // tn),
        in_specs=[
            pl.BlockSpec((tm, k), lambda i, j: (i, 0)),
            pl.BlockSpec((k, tn), lambda i, j: (0, j)),
            pl.BlockSpec((1, tn), lambda i, j: (0, j)),
        ],
        out_specs=pl.BlockSpec((tm, tn), lambda i, j: (i, j)),
        out_shape=jax.ShapeDtypeStruct((m, n), F32),
        compiler_params=_params(("parallel", "parallel")),
        name=name,
    )(a, w, b.reshape(1, n))


def _plain_kernel(a_ref, w_ref, o_ref):
    o_ref[...] = _dot(a_ref[...], w_ref[...]).astype(o_ref.dtype)


def _mm_plain(a, w, name="mm_plain"):
    m, k = a.shape
    n = w.shape[1]
    tm, tn = _mm_tiles(m, n, 0, wide=True)
    return pl.pallas_call(
        _plain_kernel,
        grid=(m // tm, n // tn),
        in_specs=[
            pl.BlockSpec((tm, k), lambda i, j: (i, 0)),
            pl.BlockSpec((k, tn), lambda i, j: (0, j)),
        ],
        out_specs=pl.BlockSpec((tm, tn), lambda i, j: (i, j)),
        out_shape=jax.ShapeDtypeStruct((m, n), BF16),
        compiler_params=_params(("parallel", "parallel")),
        name=name,
    )(a, w)


def _ffn_up_kernel(a_ref, ap_ref, an_ref, wg_ref, wv_ref, cwg_ref, cwv_ref, cbg_ref, cbv_ref,
                   o_ref, ext_ref, *, tm, n_ctx, m_total):
    i = pl.program_id(0)
    h = BF16_SUBLANES

    @pl.when(pl.program_id(1) == 0)
    def _():
        ext_ref[0:h, :] = ap_ref[...]
        ext_ref[h:h + tm, :] = a_ref[...]
        ext_ref[h + tm:h + tm + h, :] = an_ref[...]

    rows = _row_ids(i, tm)
    seq_first = (rows == 0) | (rows == n_ctx)
    seq_last = (rows == n_ctx - 1) | (rows == m_total - 1)
    a_ext = ext_ref[...]
    n_ext = tm + 2 * h

    def conv(w_ref, cw_ref, cb_ref):
        u = _dot(a_ext, w_ref[...])
        u_prev = pltpu.roll(u, 1, axis=0)[h:h + tm]
        u_next = pltpu.roll(u, n_ext - 1, axis=0)[h:h + tm]
        u_prev = jnp.where(seq_first, 0.0, u_prev)
        u_next = jnp.where(seq_last, 0.0, u_next)
        cw = cw_ref[...]
        return u_prev * cw[0:1] + u[h:h + tm] * cw[1:2] + u_next * cw[2:3] + cb_ref[...]

    gate = conv(wg_ref, cwg_ref, cbg_ref)
    val = conv(wv_ref, cwv_ref, cbv_ref)
    o_ref[...] = (_silu(gate) * val).astype(o_ref.dtype)


def _ffn_up(hn, w_up, layer, conv_w, conv_b, n_ctx):
    m, d = hn.shape
    dff = w_up.shape[2] // 2
    tm, tn = _mm_tiles(m, dff, n_ctx)
    jn = dff // tn
    h = BF16_SUBLANES
    rb = tm // h
    last_rb = m // h - 1
    return pl.pallas_call(
        functools.partial(_ffn_up_kernel, tm=tm, n_ctx=n_ctx, m_total=m),
        grid=(m // tm, jn),
        in_specs=[
            pl.BlockSpec((tm, d), lambda i, j: (i, 0)),
            pl.BlockSpec((h, d), lambda i, j: (jnp.maximum(i * rb - 1, 0), 0)),
            pl.BlockSpec((h, d), lambda i, j: (jnp.minimum((i + 1) * rb, last_rb), 0)),
            pl.BlockSpec((None, d, tn), lambda i, j: (layer, 0, j)),
            pl.BlockSpec((None, d, tn), lambda i, j: (layer, 0, jn + j)),
            pl.BlockSpec((CONV_WIDTH, tn), lambda i, j: (0, j)),
            pl.BlockSpec((CONV_WIDTH, tn), lambda i, j: (0, jn + j)),
            pl.BlockSpec((1, tn), lambda i, j: (0, j)),
            pl.BlockSpec((1, tn), lambda i, j: (0, jn + j)),
        ],
        out_specs=pl.BlockSpec((tm, tn), lambda i, j: (i, j)),
        out_shape=jax.ShapeDtypeStruct((m, dff), BF16),
        scratch_shapes=[pltpu.VMEM((tm + 2 * h, d), BF16)],
        compiler_params=_params(("parallel", "arbitrary")),
        name="ffn_up",
    )(hn, hn, hn, w_up, w_up, conv_w, conv_w, conv_b.reshape(1, -1), conv_b.reshape(1, -1))


def _conv_ffn(z, norm_g, mod, w_up, w_down, layer, conv_w, conv_b, n_ctx):
    hn = _norm_mod(z, norm_g, mod, 3, n_ctx)
    act = _ffn_up(hn, w_up, layer, conv_w, conv_b, n_ctx)
    return _mm_resid(act, w_down, z, mod, 5, n_ctx, layer=layer, name="ffn_down")


def _pool_kernel(x_ref, xp_ref, xn_ref, g_ref, sh_ref, sc_ref, o_ref, s_ref, a_ref, *, tm, n_ctx, m_total):
    i = pl.program_id(0)
    p = POOL_HALO
    d = x_ref.shape[1]
    cg = d // len(POOL_WINDOWS)
    a_ref[...] = g_ref[...] * (1.0 + sc_ref[...])

    def modulated(x):
        ms = jnp.mean(x * x, axis=-1, keepdims=True)
        return x * lax.rsqrt(ms + EPS) * a_ref[...] + sh_ref[...]

    start = i * tm
    stop = start + tm
    prev_ok = jnp.logical_and(start != 0, start != n_ctx)
    next_ok = jnp.logical_and(stop != n_ctx, stop != m_total)
    s_ref[0:p, :] = jnp.where(prev_ok, modulated(xp_ref[...]), 0.0)
    s_ref[p + tm:p + tm + p, :] = jnp.where(next_ok, modulated(xn_ref[...]), 0.0)

    def body(r, carry):
        base = pl.multiple_of(r * p, p)
        s_ref[pl.ds(p + base, p), :] = modulated(x_ref[pl.ds(base, p), :])
        return carry

    lax.fori_loop(0, tm // p, body, 0, unroll=NORM_UNROLL)

    in_ctx = start < n_ctx
    t0 = jnp.where(in_ctx, start, start - n_ctx)
    t_len = jnp.where(in_ctx, n_ctx, m_total - n_ctx)
    nr = POOL_ROWS
    for rb in range(0, tm, nr):
        t = t0 + rb + lax.broadcasted_iota(jnp.int32, (nr, 1), 0)
        for gi, w in enumerate(POOL_WINDOWS):
            cols = slice(gi * cg, (gi + 1) * cg)
            acc = s_ref[p + rb - w // 2:p + rb - w // 2 + nr, cols]
            for k in range(1 - w // 2, w - w // 2):
                acc = acc + s_ref[p + rb + k:p + rb + k + nr, cols]
            lo = jnp.maximum(t - w // 2, 0)
            hi = jnp.minimum(t + w - w // 2, t_len)
            cnt = (hi - lo).astype(F32)
            o_ref[rb:rb + nr, cols] = (acc / cnt - s_ref[p + rb:p + rb + nr, cols]).astype(o_ref.dtype)


def _pool_pre(z, g, mod, n_ctx):
    m, d = z.shape
    tm = _pick(n_ctx, (256, 128, 64, 32))
    p = POOL_HALO
    rb = tm // p
    last_rb = m // p - 1
    mod8 = jnp.repeat(mod, p, axis=0)
    vec_blk = lambda i: jnp.where(i * tm < n_ctx, 1, 0)
    return pl.pallas_call(
        functools.partial(_pool_kernel, tm=tm, n_ctx=n_ctx, m_total=m),
        grid=(m // tm,),
        in_specs=[
            pl.BlockSpec((tm, d), lambda i: (i, 0)),
            pl.BlockSpec((p, d), lambda i: (jnp.maximum(i * rb - 1, 0), 0)),
            pl.BlockSpec((p, d), lambda i: (jnp.minimum((i + 1) * rb, last_rb), 0)),
            pl.BlockSpec((p, d), lambda i: (0, 0)),
            pl.BlockSpec((p, d), lambda i: (vec_blk(i), 0)),
            pl.BlockSpec((p, d), lambda i: (vec_blk(i), 1)),
        ],
        out_specs=pl.BlockSpec((tm, d), lambda i: (i, 0)),
        out_shape=jax.ShapeDtypeStruct((m, d), BF16),
        scratch_shapes=[pltpu.VMEM((tm + 2 * p, d), F32), pltpu.VMEM((p, d), F32)],
        compiler_params=_params(("parallel",)),
        name="pool_pre",
    )(z, z, z, jnp.broadcast_to(g.reshape(1, d), (p, d)), mod8, mod8)


def _pair_cast_kernel(w_ref, o_ref, *, hd, scale):
    r = lax.broadcasted_iota(jnp.int32, (hd, hd), 0)
    c = lax.broadcasted_iota(jnp.int32, (hd, hd), 1)
    src_col = jnp.where(c < hd // 2, 2 * c, 2 * (c - hd // 2) + 1)
    perm = jnp.where(r == src_col, 1.0, 0.0).astype(BF16)
    for hh in range(w_ref.shape[1] // hd):
        cols = slice(hh * hd, (hh + 1) * hd)
        wh = (w_ref[:, cols] * scale).astype(BF16)
        o_ref[:, cols] = _dot(wh, perm).astype(BF16)


def _pair_cast(w, hd, scale=1.0):
    k, n = w.shape
    tk = _pick(k, (1024, 512, 256, 128))
    tn = _pick(n, (1024, 512, 256, 128))
    return pl.pallas_call(
        functools.partial(_pair_cast_kernel, hd=hd, scale=scale),
        grid=(k // tk, n // tn),
        in_specs=[pl.BlockSpec((tk, tn), lambda i, j: (i, j))],
        out_specs=pl.BlockSpec((tk, tn), lambda i, j: (i, j)),
        out_shape=jax.ShapeDtypeStruct((k, n), BF16),
        compiler_params=_params(("parallel", "parallel")),
        name="pair_cast",
    )(w)


def _head_proj_kernel(a_ref, w_ref, g_ref, c_ref, s_ref, o_ref, *, hd, normalize):
    acc = _dot(a_ref[...], w_ref[...])
    cos = c_ref[...]
    sin = s_ref[...]
    mean_w = jnp.full((2 * hd, hd), 1.0 / hd, BF16)
    for hh in range(acc.shape[1] // hd):
        cols = slice(hh * hd, (hh + 1) * hd)
        y = acc[:, cols]
        if normalize:
            sq = y * y
            hi = sq.astype(BF16)
            lo = (sq - hi.astype(F32)).astype(BF16)
            ms = _dot(jnp.concatenate([hi, lo], axis=1), mean_w)
            y = y * lax.rsqrt(ms + EPS) * g_ref[:, cols]
        o_ref[:, cols] = (y * cos + pltpu.roll(y, hd // 2, axis=1) * sin).astype(o_ref.dtype)


def _head_proj(hn, w, gain, cos, sin, hd, normalize, name):
    m, d = hn.shape
    n = w.shape[1]
    tm, tn = _mm_tiles(m, n, 0, wide=True)
    tn = max(tn, hd)
    return pl.pallas_call(
        functools.partial(_head_proj_kernel, hd=hd, normalize=normalize),
        grid=(m // tm, n // tn),
        in_specs=[
            pl.BlockSpec((tm, d), lambda i, j: (i, 0)),
            pl.BlockSpec((d, tn), lambda i, j: (0, j)),
            pl.BlockSpec((1, tn), lambda i, j: (0, j)),
            pl.BlockSpec((tm, hd), lambda i, j: (i, 0)),
            pl.BlockSpec((tm, hd), lambda i, j: (i, 0)),
        ],
        out_specs=pl.BlockSpec((tm, tn), lambda i, j: (i, j)),
        out_shape=jax.ShapeDtypeStruct((m, n), BF16),
        compiler_params=_params(("parallel", "parallel")),
        name=name,
    )(hn, w, gain.reshape(1, n), cos, sin)


def _flash_kernel(q_ref, kt_ref, v_ref, o_ref, qs_ref, vx_ref, *, tq, chunk):
    hd = ATTN_HEAD_DIM
    n_keys = v_ref.shape[0]

    @pl.when(pl.program_id(1) == 0)
    def _():
        vx_ref[:, 0:hd] = v_ref[...]
        lane = lax.broadcasted_iota(jnp.int32, (n_keys, hd), 1)
        vx_ref[:, hd:2 * hd] = jnp.where(lane == 0, 1.0, 0.0).astype(BF16)

    for gi in range(ATTN_GROUP):
        qs_ref[gi * tq:(gi + 1) * tq, :] = q_ref[:, gi * hd:(gi + 1) * hd]
    q = qs_ref[...]
    m_run = None
    acc = None
    for start in range(0, n_keys, chunk):
        size = min(chunk, n_keys - start)
        s = _dot(q, kt_ref[:, start:start + size])
        m_c = jnp.max(s, axis=-1, keepdims=True)
        m_new = m_c if m_run is None else jnp.maximum(m_run, m_c)
        p = jnp.exp2(s - m_new).astype(BF16)
        pv = _dot(p, vx_ref[start:start + size, :])
        acc = pv if acc is None else jnp.exp2(m_run - m_new) * acc + pv
        m_run = m_new
    out = acc[:, 0:hd] / acc[:, hd:hd + 1]
    for gi in range(ATTN_GROUP):
        o_ref[:, gi * hd:(gi + 1) * hd] = out[gi * tq:(gi + 1) * tq].astype(o_ref.dtype)


def _flash_part(q, kt, v, row0, n_rows, n_keys):
    hd = ATTN_HEAD_DIM
    kvh = v.shape[1] // hd
    tq = _pick(n_rows, (256, 128, 64, 32, 16))
    chunk = min(n_keys, FLASH_KEY_CHUNK)
    gw = ATTN_GROUP * hd
    r0 = row0 // tq
    return pl.pallas_call(
        functools.partial(_flash_kernel, tq=tq, chunk=chunk),
        grid=(kvh, n_rows // tq),
        in_specs=[
            pl.BlockSpec((tq, gw), lambda g, i: (r0 + i, g)),
            pl.BlockSpec((hd, n_keys), lambda g, i: (g, 0)),
            pl.BlockSpec((n_keys, hd), lambda g, i: (0, g)),
        ],
        out_specs=pl.BlockSpec((tq, gw), lambda g, i: (i, g)),
        out_shape=jax.ShapeDtypeStruct((n_rows, q.shape[1]), BF16),
        scratch_shapes=[
            pltpu.VMEM((ATTN_GROUP * tq, hd), BF16),
            pltpu.VMEM((n_keys, 2 * hd), BF16),
        ],
        compiler_params=_params(("parallel", "arbitrary")),
        name="flash_attn",
    )(q, kt, v)


def _flash(q, k, v, n_ctx):
    m = q.shape[0]
    kt = k.T
    o_lat = _flash_part(q, kt, v, n_ctx, m - n_ctx, m)
    o_ctx = _flash_part(q, kt, v, 0, n_ctx, n_ctx)
    return jnp.concatenate([o_ctx, o_lat], axis=0)


def _gate_prep_kernel(x_ref, o_ref, *, n_heads):
    x = x_ref[...]
    rows = x.shape[0]
    col = lax.broadcasted_iota(jnp.int32, x.shape, 1)
    is_f = jnp.logical_and((col // n_heads) % 2 == 1, col < 4 * n_heads)
    backward = col >= 2 * n_heads
    logf = _log_sigmoid(x)
    r = lax.broadcasted_iota(jnp.int32, (rows, rows), 0)
    c = lax.broadcasted_iota(jnp.int32, (rows, rows), 1)
    tril = (c <= r).astype(F32)
    triu = (c >= r).astype(F32)
    fwd = jnp.dot(tril, logf, preferred_element_type=F32, precision=lax.Precision.HIGHEST)
    bwd = jnp.dot(triu, logf, preferred_element_type=F32, precision=lax.Precision.HIGHEST)
    o_ref[...] = jnp.where(is_f, jnp.where(backward, bwd, fwd), x)


def _gate_prep(gates, n_heads):
    m, n = gates.shape
    ch = SCAN_CHUNK
    return pl.pallas_call(
        functools.partial(_gate_prep_kernel, n_heads=n_heads),
        grid=(m // ch,),
        in_specs=[pl.BlockSpec((ch, n), lambda c: (c, 0))],
        out_specs=pl.BlockSpec((ch, n), lambda c: (c, 0)),
        out_shape=jax.ShapeDtypeStruct((m, n), F32),
        compiler_params=_params(("parallel",)),
        name="gate_prep",
    )(gates)


def _scan_block(n_blk):
    def blk(d, c):
        return jnp.where(c == 0, 0, jnp.where(d == 0, c, n_blk - c))
    return blk


def _mlstm_kernel(q_ref, k_ref, v_ref, gc_ref, ir_ref, br_ref, o_ref, s_ref, n_ref, m_ref, *, n_heads, hp):
    d = pl.program_id(0)
    ch = q_ref.shape[0]
    dk = q_ref.shape[1] // hp
    dv = v_ref.shape[1] // hp

    @pl.when(pl.program_id(2) == 0)
    def _():
        s_ref[...] = jnp.zeros(s_ref.shape, F32)
        n_ref[...] = jnp.zeros(n_ref.shape, F32)
        m_ref[...] = jnp.zeros(m_ref.shape, F32)

    gc = gc_ref[...]
    lane = lax.broadcasted_iota(jnp.int32, gc.shape, 1)
    r = lax.broadcasted_iota(jnp.int32, (ch, ch), 0)
    c = lax.broadcasted_iota(jnp.int32, (ch, ch), 1)
    sgn = 1 - 2 * d
    seen = (r - c) * sgn >= 0

    for hl in range(hp):
        q = q_ref[:, hl * dk:(hl + 1) * dk]
        k = k_ref[:, hl * dk:(hl + 1) * dk]
        v = v_ref[:, hl * dv:(hl + 1) * dv]
        base = d * 2 * n_heads + pl.program_id(1) * hp + hl
        i_col = jnp.sum(jnp.where(lane == base, gc, 0.0), axis=1, keepdims=True)
        b_col = jnp.sum(jnp.where(lane == base + n_heads, gc, 0.0), axis=1, keepdims=True)
        i_row = ir_ref[hl]
        b_row = br_ref[hl]
        m_prev = m_ref[hl]
        s_prev = s_ref[hl]
        n_prev = n_ref[hl]

        log_d = jnp.where(seen, b_col - b_row + i_row, -jnp.inf)
        log_prev = b_col + m_prev
        m_t = jnp.maximum(log_prev, jnp.max(log_d, axis=-1, keepdims=True))
        qk = lax.dot_general(q, k, (((1,), (1,)), ((), ())), preferred_element_type=F32)
        s = qk * jnp.exp(log_d - m_t)
        w_prev = jnp.exp(log_prev - m_t)
        num = _dot(s.astype(BF16), v) + w_prev * _dot(q, s_prev.astype(BF16))
        den = (jnp.sum(s, axis=-1, keepdims=True)
               + w_prev * jnp.sum(q.astype(F32) * n_prev, axis=-1, keepdims=True))
        o_ref[:, hl * dv:(hl + 1) * dv] = num / jnp.maximum(jnp.abs(den), jnp.exp(-m_t))

        b_last = jnp.where(d == 0, b_row[:, ch - 1:ch], b_row[:, 0:1])
        log_w_row = b_last - b_row + i_row
        m_new = jnp.maximum(b_last + m_prev, jnp.max(log_w_row, axis=-1, keepdims=True))
        w_col = jnp.exp(b_last - b_col + i_col - m_new)
        decay = jnp.exp(b_last + m_prev - m_new)
        wv = (w_col * v.astype(F32)).astype(BF16)
        s_ref[hl] = decay * s_prev + lax.dot_general(k, wv, (((0,), (0,)), ((), ())),
                                                     preferred_element_type=F32)
        n_ref[hl] = decay * n_prev + jnp.sum(w_col * k.astype(F32), axis=0, keepdims=True)
        m_ref[hl] = m_new


def _mlstm_scan(q, k, v, g2, g2t, n_heads, dk, dv):
    m = q.shape[0]
    ch = SCAN_CHUNK
    n_blk = m // ch
    blk = _scan_block(n_blk)
    hp = SCAN_HEADS_PER_STEP
    hb = n_heads // hp
    return pl.pallas_call(
        functools.partial(_mlstm_kernel, n_heads=n_heads, hp=hp),
        grid=(2, hb, n_blk),
        in_specs=[
            pl.BlockSpec((ch, hp * dk), lambda d, h, c: (blk(d, c), h)),
            pl.BlockSpec((ch, hp * dk), lambda d, h, c: (blk(d, c), h)),
            pl.BlockSpec((ch, hp * dv), lambda d, h, c: (blk(d, c), h)),
            pl.BlockSpec((ch, g2.shape[1]), lambda d, h, c: (blk(d, c), 0)),
            pl.BlockSpec((hp, 1, ch), lambda d, h, c: (d * 2 * hb + h, 0, blk(d, c))),
            pl.BlockSpec((hp, 1, ch), lambda d, h, c: (d * 2 * hb + hb + h, 0, blk(d, c))),
        ],
        out_specs=pl.BlockSpec((None, ch, hp * dv), lambda d, h, c: (d, blk(d, c), h)),
        out_shape=jax.ShapeDtypeStruct((2, m, n_heads * dv), F32),
        scratch_shapes=[
            pltpu.VMEM((hp, dk, dv), F32),
            pltpu.VMEM((hp, 1, dk), F32),
            pltpu.VMEM((hp, 1, 1), F32),
        ],
        compiler_params=_params(("parallel", "parallel", "arbitrary")),
        name="mlstm_scan",
    )(q, k, v, g2, g2t, g2t)


def _gated_norm_kernel(a_ref, w_ref, hf_ref, hb_ref, g_ref, o_ref, *, hw, act):
    acc = _dot(a_ref[...], w_ref[...])
    for hh in range(acc.shape[1] // hw):
        cols = slice(hh * hw, (hh + 1) * hw)
        y = hf_ref[:, cols] + hb_ref[:, cols]
        ms = jnp.mean(y * y, axis=-1, keepdims=True)
        y = y * lax.rsqrt(ms + EPS) * g_ref[:, cols]
        o_ref[:, cols] = (act(acc[:, cols]) * y).astype(o_ref.dtype)


def _mm_gated_norm(hn, w, hfb, gain, hw, act, name):
    m, d = hn.shape
    n = w.shape[1]
    tm, tn = _mm_tiles(m, n, 0, wide=True)
    tn = max(tn, hw)
    return pl.pallas_call(
        functools.partial(_gated_norm_kernel, hw=hw, act=act),
        grid=(m // tm, n // tn),
        in_specs=[
            pl.BlockSpec((tm, d), lambda i, j: (i, 0)),
            pl.BlockSpec((d, tn), lambda i, j: (0, j)),
            pl.BlockSpec((None, tm, tn), lambda i, j: (0, i, j)),
            pl.BlockSpec((None, tm, tn), lambda i, j: (1, i, j)),
            pl.BlockSpec((1, tn), lambda i, j: (0, j)),
        ],
        out_specs=pl.BlockSpec((tm, tn), lambda i, j: (i, j)),
        out_shape=jax.ShapeDtypeStruct((m, n), BF16),
        compiler_params=_params(("parallel", "parallel")),
        name=name,
    )(hn, w, hfb, hfb, gain.reshape(1, n))


def _ret_kernel(q_ref, k_ref, v_ref, lg_ref, o_ref, s_ref, dec_ref, xi_ref, zeta_ref, *, hp):
    d = pl.program_id(0)
    ch = q_ref.shape[0]
    hd = q_ref.shape[1] // hp

    @pl.when(pl.program_id(2) == 0)
    def _():
        s_ref[...] = jnp.zeros(s_ref.shape, F32)
        r = lax.broadcasted_iota(jnp.int32, (ch, ch), 0)
        c = lax.broadcasted_iota(jnp.int32, (ch, ch), 1)
        sgn = 1 - 2 * d
        rel = (r - c) * sgn
        seen = rel >= 0
        t = lax.broadcasted_iota(jnp.int32, (ch, hd), 0)
        pos = jnp.where(d == 0, t, ch - 1 - t).astype(F32)
        for hl in range(hp):
            lg = _log_sigmoid(lg_ref[hl])[:, 0:1]
            dec_ref[hl] = jnp.where(seen, jnp.exp(lg * jnp.where(seen, rel, 0).astype(F32)), 0.0)
            xi_ref[hl] = jnp.exp(lg * (pos + 1.0))
            zeta_ref[hl] = jnp.exp(lg * (ch - 1.0 - pos))

    for hl in range(hp):
        q = q_ref[:, hl * hd:(hl + 1) * hd]
        k = k_ref[:, hl * hd:(hl + 1) * hd]
        v = v_ref[:, hl * hd:(hl + 1) * hd]
        s_prev = s_ref[hl]
        gamma_chunk = jnp.exp(_log_sigmoid(lg_ref[hl])[:, 0:1] * ch)
        qk = lax.dot_general(q, k, (((1,), (1,)), ((), ())), preferred_element_type=F32)
        sc = (qk * dec_ref[hl]).astype(BF16)
        o_ref[:, hl * hd:(hl + 1) * hd] = _dot(sc, v) + _dot(q, s_prev.astype(BF16)) * xi_ref[hl]
        kz = (k.astype(F32) * zeta_ref[hl]).astype(BF16)
        s_ref[hl] = gamma_chunk * s_prev + lax.dot_general(kz, v, (((0,), (0,)), ((), ())),
                                                           preferred_element_type=F32)


def _ret_scan(q, k, v, decay_logit, n_heads, hd):
    m = q.shape[0]
    ch = SCAN_CHUNK
    n_blk = m // ch
    blk = _scan_block(n_blk)
    hp = RET_HEADS_PER_STEP
    hb = n_heads // hp
    lg = jnp.broadcast_to(decay_logit.astype(F32).reshape(2 * n_heads, 1, 1), (2 * n_heads, 1, LANES))
    return pl.pallas_call(
        functools.partial(_ret_kernel, hp=hp),
        grid=(2, hb, n_blk),
        in_specs=[
            pl.BlockSpec((ch, hp * hd), lambda d, h, c: (blk(d, c), h)),
            pl.BlockSpec((ch, hp * hd), lambda d, h, c: (blk(d, c), h)),
            pl.BlockSpec((ch, hp * hd), lambda d, h, c: (blk(d, c), h)),
            pl.BlockSpec((hp, 1, LANES), lambda d, h, c: (d * hb + h, 0, 0)),
        ],
        out_specs=pl.BlockSpec((None, ch, hp * hd), lambda d, h, c: (d, blk(d, c), h)),
        out_shape=jax.ShapeDtypeStruct((2, m, n_heads * hd), F32),
        scratch_shapes=[
            pltpu.VMEM((hp, hd, hd), F32),
            pltpu.VMEM((hp, ch, ch), F32),
            pltpu.VMEM((hp, ch, hd), F32),
            pltpu.VMEM((hp, ch, hd), F32),
        ],
        compiler_params=_params(("parallel", "parallel", "arbitrary")),
        name="ret_scan",
    )(q, k, v, lg)


def _split_pairs(w, n_heads, hd):
    lead = w.shape[:-1]
    return w.reshape(*lead, n_heads, hd // 2, 2).swapaxes(-1, -2).reshape(*lead, n_heads * hd)


def _rope_tables(n_lat, n_ctx, head_dim):
    t = jnp.arange(n_lat)
    row = (t // GRID_W).astype(F32)
    col = (t % GRID_W).astype(F32)
    axis_dim = head_dim // 2
    inv_freq = 1.0 / (ROPE_THETA ** (jnp.arange(0, axis_dim, 2, dtype=F32) / axis_dim))
    ang = jnp.concatenate([row[:, None] * inv_freq, col[:, None] * inv_freq], axis=-1)
    cos = jnp.concatenate([jnp.ones((n_ctx, head_dim // 2), F32), jnp.cos(ang)], axis=0)
    sin = jnp.concatenate([jnp.zeros((n_ctx, head_dim // 2), F32), jnp.sin(ang)], axis=0)
    return jnp.concatenate([cos, cos], axis=1), jnp.concatenate([-sin, sin], axis=1)


def kernel(x, c, ctx, c_ctx, ada_w, ada_b, norm1_g, norm2_g, ffn_w_up, ffn_conv_w, ffn_conv_b, ffn_w_down, pool_w, pool_scale, attn_wq, attn_wk, attn_wv, attn_wo, attn_q_norm, attn_k_norm, mlstm_wq, mlstm_wk, mlstm_wv, mlstm_w_gates, mlstm_b_gates, mlstm_w_ogate, mlstm_out_norm, mlstm_wo, ret_wq, ret_wk, ret_wv, ret_wg, ret_decay_logit, ret_out_norm, ret_wo):
    batch, n_lat, d = x.shape
    n_ctx = ctx.shape[1]
    depth = ada_w.shape[0]
    assert batch == 1 and n_ctx % BF16_SUBLANES == 0 and n_lat % n_ctx == 0

    z = jnp.concatenate([ctx[0], x[0]], axis=0)
    cc = jnp.zeros((8, d), F32).at[0].set(c[0]).at[1].set(c_ctx)
    mods = _ada_mods(cc, ada_w, ada_b)
    w_up = ffn_w_up.astype(BF16)
    w_down = ffn_w_down.astype(BF16)

    for i in range(depth):
        kind, j = i % 4, i // 4
        mod = mods[i, :2]
        if kind == 0:
            pooled = _pool_pre(z, norm1_g[i], mod, n_ctx)
            z = _mm_resid(pooled, pool_w[j].astype(BF16), z, mod, 2, n_ctx, colscale=pool_scale[j],
                          groups=len(POOL_WINDOWS), name="pool_mix")
        elif kind == 1:
            hd = ATTN_HEAD_DIM
            qh = attn_wq.shape[2] // hd
            kvh = attn_wk.shape[2] // hd
            hn = _norm_mod(z, norm1_g[i], mod, 0, n_ctx)
            cos, sin = _rope_tables(n_lat, n_ctx, hd)
            gq = jnp.tile(_split_pairs(attn_q_norm[j], 1, hd) * (hd ** -0.5 * LOG2_E), qh)
            gk = jnp.tile(_split_pairs(attn_k_norm[j], 1, hd), kvh)
            q = _head_proj(hn, _pair_cast(attn_wq[j], hd), gq, cos, sin, hd, True, "attn_q")
            k = _head_proj(hn, _pair_cast(attn_wk[j], hd), gk, cos, sin, hd, True, "attn_k")
            v = _mm_plain(hn, attn_wv[j].astype(BF16), name="attn_v")
            o = _flash(q, k, v, n_ctx)
            z = _mm_resid(o, attn_wo[j].astype(BF16), z, mod, 2, n_ctx, name="attn_out")
        elif kind == 2:
            nh = MLSTM_HEADS
            dk = mlstm_wq.shape[2] // nh
            dv = mlstm_wv.shape[2] // nh
            hn = _norm_mod(z, norm1_g[i], mod, 0, n_ctx)
            q = _mm_plain(hn, mlstm_wq[j].astype(BF16), name="mlstm_q")
            k = _mm_plain(hn, (mlstm_wk[j] * dk ** -0.5).astype(BF16), name="mlstm_k")
            v = _mm_plain(hn, mlstm_wv[j].astype(BF16), name="mlstm_v")
            wg = jnp.zeros((d, LANES), F32).at[:, :4 * nh].set(mlstm_w_gates[j]).astype(BF16)
            bg = jnp.zeros((LANES,), F32).at[:4 * nh].set(mlstm_b_gates[j])
            g2 = _gate_prep(_mm_bias(hn, wg, bg, name="mlstm_gates"), nh)
            g2t = g2.T.reshape(LANES, 1, -1)
            hfb = _mlstm_scan(q, k, v, g2, g2t, nh, dk, dv)
            gated = _mm_gated_norm(hn, mlstm_w_ogate[j].astype(BF16), hfb, mlstm_out_norm[j], dv,
                                   _sigmoid, "mlstm_ogate")
            z = _mm_resid(gated, mlstm_wo[j].astype(BF16), z, mod, 2, n_ctx, name="mlstm_out")
        else:
            nh = RET_HEADS
            hd = ret_wq.shape[2] // nh
            hn = _norm_mod(z, norm1_g[i], mod, 0, n_ctx)
            cos, sin = _rope_tables(n_lat, n_ctx, hd)
            ones = jnp.ones((nh * hd,), F32)
            q = _head_proj(hn, _pair_cast(ret_wq[j], hd), ones, cos, sin, hd, False, "ret_q")
            k = _head_proj(hn, _pair_cast(ret_wk[j], hd, hd ** -0.5), ones, cos, sin, hd, False, "ret_k")
            v = _mm_plain(hn, ret_wv[j].astype(BF16), name="ret_v")
            ofb = _ret_scan(q, k, v, ret_decay_logit[j], nh, hd)
            gated = _mm_gated_norm(hn, ret_wg[j].astype(BF16), ofb, ret_out_norm[j], hd, _silu, "ret_gate")
            z = _mm_resid(gated, ret_wo[j].astype(BF16), z, mod, 2, n_ctx, name="ret_out")
        z = _conv_ffn(z, norm2_g[i], mod, w_up, w_down, i, ffn_conv_w[i], ffn_conv_b[i], n_ctx)
    return z[n_ctx:][None]
```
